```python
import jax
import jax.numpy as jnp
from jax import lax
import numpy as np

D_MODEL = 1024
BATCH = 4
SEQ = 8192
DEPTH = 1
DEC_BATCH = 32
DEC_SEQ = 4
PAST_LEN = 16384
PAGE_SIZE = 128

HEAD_DIM = 64
N_HEADS_SB = 8
N_HEADS_NSA = 8
NSA_KV_HEADS = 2
NSA_GROUP = N_HEADS_NSA // NSA_KV_HEADS
SB_WIDTH = N_HEADS_SB * HEAD_DIM
NSA_WIDTH = N_HEADS_NSA * HEAD_DIM
MIX_WIDTH = SB_WIDTH + NSA_WIDTH
KV_WIDTH = NSA_KV_HEADS * HEAD_DIM
N_GATES = 3 * N_HEADS_NSA
IN_SIZES = (SB_WIDTH, SB_WIDTH, SB_WIDTH, NSA_WIDTH) + (KV_WIDTH,) * 6 + (N_GATES,)
IN_WIDTH = sum(IN_SIZES)
CMP_BLOCK = 32
CMP_STRIDE = 16
CMP_HIDDEN = 2 * HEAD_DIM
SLC_BLOCK = 64
SLC_TOP_N = 16
WINDOW = 512
Q_BLOCK = 128
N_EXPERTS = 32
TOP_K = 4
D_FF = D_MODEL
SWIGLU_LIMIT = 7.0
SWIGLU_ALPHA = 1.702
MOE_ROW_BLOCK = 256
ROPE_THETA = 10000.0
EPS = 1e-6
NEG = -1e30
FORCE_BONUS = 1e3

kernel_name = 'stickbreak_nsa_hybrid_moe_step'


def rmsnorm(x, g):
    xf = x.astype(jnp.float32)
    xf = xf * lax.rsqrt(jnp.mean(xf * xf, axis=-1, keepdims=True) + EPS)
    return (xf * g.astype(jnp.float32)).astype(x.dtype)


def rope(x, pos):
    half = HEAD_DIM // 2
    inv_freq = jnp.power(ROPE_THETA, -jnp.arange(half, dtype=jnp.float32) / half)
    ang = pos.astype(jnp.float32)[:, None] * inv_freq[None, :]
    cos = jnp.cos(ang)[None, :, None, :]
    sin = jnp.sin(ang)[None, :, None, :]
    xf = x.astype(jnp.float32)
    x1, x2 = xf[..., :half], xf[..., half:]
    return jnp.concatenate([x1 * cos - x2 * sin, x2 * cos + x1 * sin], axis=-1).astype(x.dtype)


def masked_softmax(s, mask):
    s = jnp.where(mask, s, NEG)
    m = jnp.max(s, axis=-1, keepdims=True)
    e = jnp.where(mask, jnp.exp(s - m), 0.0)
    return e / jnp.maximum(jnp.sum(e, axis=-1, keepdims=True), 1.0)


def sb_attend(q, q_pos, segments):
    scale = HEAD_DIM ** -0.5
    z = jnp.concatenate([jnp.einsum('bqhd,bkhd->bhqk', q, k).astype(jnp.float32) for k, _, _ in segments], axis=-1) * scale
    k_pos = jnp.concatenate([p for _, _, p in segments])
    mask = k_pos[None, :] < q_pos[:, None]
    log_keep = jnp.where(mask, jax.nn.log_sigmoid(-z), 0.0)
    log_stick = lax.cumsum(log_keep, axis=3, reverse=True) - log_keep
    a = jnp.where(mask, jnp.exp(jax.nn.log_sigmoid(z) + log_stick), 0.0).astype(q.dtype)
    outs = []
    off = 0
    for k, v, _ in segments:
        n = k.shape[1]
        outs.append(jnp.einsum('bhqk,bkhd->bqhd', a[..., off:off + n], v))
        off += n
    return sum(outs[1:], outs[0])


def sb_prompt(q, k, v):
    b, t = q.shape[:2]
    nb = t // Q_BLOCK
    k_pos = jnp.arange(t)
    qb = q.reshape(b, nb, Q_BLOCK, N_HEADS_SB, HEAD_DIM).swapaxes(0, 1)

    def block(args):
        qi, i = args
        q_pos = i * Q_BLOCK + jnp.arange(Q_BLOCK)
        return sb_attend(qi, q_pos, [(k, v, k_pos)])

    out = lax.map(block, (qb, jnp.arange(nb)))
    return out.swapaxes(0, 1).reshape(b, t, SB_WIDTH)


def compress(rows, pe, w1, w2):
    b, t, g, d = rows.shape
    t_pad = -(-t // CMP_STRIDE) * CMP_STRIDE
    rows = jnp.pad(rows, ((0, 0), (0, t_pad - t), (0, 0), (0, 0)))
    n_chunk = t_pad // CMP_STRIDE
    ratio = CMP_BLOCK // CMP_STRIDE
    n_cmp = n_chunk - ratio + 1
    ch = rows.reshape(b, n_chunk, CMP_STRIDE, g, d)
    blocks = jnp.concatenate([ch[:, r:r + n_cmp] for r in range(ratio)], axis=2)
    blocks = blocks + pe[:, None, :]
    flat = blocks.transpose(0, 1, 3, 2, 4).reshape(b, n_cmp, g, CMP_BLOCK * d)
    out = jax.nn.silu(flat @ w1) @ w2
    end = jnp.arange(n_cmp) * CMP_STRIDE + CMP_BLOCK - 1
    return out, end


def compress_kv(ck, cv, pe_k, w_ck1, w_ck2, pe_v, w_cv1, w_cv2):
    kc, c_end = compress(ck, pe_k, w_ck1, w_ck2)
    vc, _ = compress(cv, pe_v, w_cv1, w_cv2)
    return rope(kc, c_end), vc, c_end


def cmp_to_slc(n_cmp, n_slc):
    cs = jnp.arange(n_cmp) * CMP_STRIDE
    ss = jnp.arange(n_slc) * SLC_BLOCK
    return ((cs[:, None] < ss[None, :] + SLC_BLOCK) & (cs[:, None] + CMP_BLOCK > ss[None, :])).astype(jnp.float32)


def nsa_attend(q, q_pos, kc, vc, c_end, ks, vs, kw, vw, w_pos, gates):
    b, tq = q.shape[:2]
    scale = HEAD_DIM ** -0.5
    qg = q.reshape(b, tq, NSA_KV_HEADS, NSA_GROUP, HEAD_DIM)
    s = jnp.einsum('bqgrd,bngd->bgrqn', qg, kc).astype(jnp.float32) * scale
    p_cmp = masked_softmax(s, c_end[None, :] <= q_pos[:, None])
    o_cmp = jnp.einsum('bgrqn,bngd->bqgrd', p_cmp.astype(vc.dtype), vc)
    tk = ks.shape[1]
    n_slc = -(-tk // SLC_BLOCK)
    n_top = min(SLC_TOP_N, n_slc)
    imp = jnp.einsum('bgrqn,ns->bgqs', p_cmp, cmp_to_slc(kc.shape[1], n_slc))
    blk = jnp.arange(n_slc)[None, :]
    cur = (q_pos // SLC_BLOCK)[:, None]
    forced = (blk == 0) | (blk == cur) | (blk == cur - 1)
    imp = jnp.where(blk * SLC_BLOCK <= q_pos[:, None], imp + jnp.where(forced, FORCE_BONUS, 0.0), NEG)
    _, idx = lax.top_k(imp, n_top)
    pad = ((0, 0), (0, n_slc * SLC_BLOCK - tk), (0, 0), (0, 0))

    def to_blocks(t):
        return jnp.pad(t, pad).reshape(b, n_slc, SLC_BLOCK, NSA_KV_HEADS, HEAD_DIM).transpose(0, 3, 1, 2, 4)

    take = jax.vmap(jax.vmap(lambda blocks, ids: blocks[ids]))
    n_keys = n_top * SLC_BLOCK
    kg = take(to_blocks(ks), idx).reshape(b, NSA_KV_HEADS, tq, n_keys, HEAD_DIM)
    vg = take(to_blocks(vs), idx).reshape(b, NSA_KV_HEADS, tq, n_keys, HEAD_DIM)
    k_pos = (idx[..., None] * SLC_BLOCK + jnp.arange(SLC_BLOCK)).reshape(b, NSA_KV_HEADS, 1, tq, n_keys)
    s = jnp.einsum('bqgrd,bgqmd->bgrqm', qg, kg).astype(jnp.float32) * scale
    p = masked_softmax(s, k_pos <= q_pos[:, None])
    o_slc = jnp.einsum('bgrqm,bgqmd->bqgrd', p.astype(vg.dtype), vg)
    s = jnp.einsum('bqgrd,bwgd->bgrqw', qg, kw).astype(jnp.float32) * scale
    dist = q_pos[:, None] - w_pos[None, :]
    p = masked_softmax(s, (w_pos[None, :] >= 0) & (dist >= 0) & (dist < WINDOW))
    o_win = jnp.einsum('bgrqw,bwgd->bqgrd', p.astype(vw.dtype), vw)
    gt = gates.reshape(b, tq, NSA_KV_HEADS, NSA_GROUP, 3)
    o = gt[..., 0:1] * o_cmp + gt[..., 1:2] * o_slc + gt[..., 2:3] * o_win
    return o.reshape(b, tq, NSA_WIDTH)


def nsa_prompt(nq, gates, kc, vc, c_end, ks, vs, kw, vw):
    b, t = nq.shape[:2]
    nb = t // Q_BLOCK
    band = WINDOW + Q_BLOCK
    pad = ((0, 0), (WINDOW, 0), (0, 0), (0, 0))
    kw_pad, vw_pad = jnp.pad(kw, pad), jnp.pad(vw, pad)
    qb = nq.reshape(b, nb, Q_BLOCK, N_HEADS_NSA, HEAD_DIM).swapaxes(0, 1)
    gb = gates.reshape(b, nb, Q_BLOCK, N_HEADS_NSA, 3).swapaxes(0, 1)

    def block(args):
        qi, gi, i = args
        s0 = i * Q_BLOCK
        q_pos = s0 + jnp.arange(Q_BLOCK)
        kwi = lax.dynamic_slice_in_dim(kw_pad, s0, band, axis=1)
        vwi = lax.dynamic_slice_in_dim(vw_pad, s0, band, axis=1)
        w_pos = s0 - WINDOW + jnp.arange(band)
        return nsa_attend(qi, q_pos, kc, vc, c_end, ks, vs, kwi, vwi, w_pos, gi)

    out = lax.map(block, (qb, gb, jnp.arange(nb)))
    return out.swapaxes(0, 1).reshape(b, t, NSA_WIDTH)


def moe(h, w_router, b_router, w_e1, b_e1, w_e2, b_e2):
    n, d = h.shape
    a = n * TOP_K
    logits = (h @ w_router + b_router).astype(jnp.float32)
    top_val, top_idx = lax.top_k(logits, TOP_K)
    gate = jax.nn.softmax(top_val, axis=-1)
    e_flat = top_idx.reshape(a)
    tok_flat = jnp.repeat(jnp.arange(n, dtype=jnp.int32), TOP_K)
    g_flat = gate.reshape(a)
    rb = MOE_ROW_BLOCK if a >= MOE_ROW_BLOCK * N_EXPERTS else max(8, a // N_EXPERTS)
    order = jnp.argsort(e_flat)
    e_sorted = e_flat[order]
    counts = jnp.bincount(e_flat, length=N_EXPERTS)
    padded = (counts + rb - 1) // rb * rb
    start = jnp.cumsum(counts) - counts
    pad_end = jnp.cumsum(padded)
    pad_start = pad_end - padded
    dest = pad_start[e_sorted] + jnp.arange(a) - start[e_sorted]
    n_blocks = -(-a // rb) + N_EXPERTS
    rows = n_blocks * rb
    row_tok = jnp.full((rows,), n, jnp.int32).at[dest].set(tok_flat[order])
    row_gate = jnp.zeros((rows,), jnp.float32).at[dest].set(g_flat[order])
    block_expert = jnp.minimum(jnp.searchsorted(pad_end, jnp.arange(n_blocks) * rb, side='right'), N_EXPERTS - 1)
    xs = jnp.concatenate([h, jnp.zeros((1, d), h.dtype)])[row_tok].reshape(n_blocks, rb, d)

    def expert_block(args):
        xb, e = args
        gu = xb @ w_e1[e] + b_e1[e]
        g, u = gu[:, :D_FF], gu[:, D_FF:]
        g = jnp.minimum(g, SWIGLU_LIMIT)
        u = jnp.clip(u, -SWIGLU_LIMIT, SWIGLU_LIMIT)
        return ((u + 1.0) * g * jax.nn.sigmoid(SWIGLU_ALPHA * g)) @ w_e2[e] + b_e2[e]

    ys = lax.map(expert_block, (xs, block_expert)).reshape(rows, d)
    ys = ys * row_gate[:, None].astype(ys.dtype)
    return jnp.zeros((n + 1, d), ys.dtype).at[row_tok].add(ys)[:n]


def mix_in(x, c, pos, lw):
    (w_ada, b_ada, g_attn, w_in, g_sb_out, g_nsa_out, w_out, pe_k, w_ck1, w_ck2, pe_v, w_cv1, w_cv2,
     g_ffn, w_router, b_router, w_e1, b_e1, w_e2, b_e2) = lw
    b, t, _ = x.shape
    m = jax.nn.silu(c) @ w_ada + b_ada
    mods = [u[:, None, :] for u in jnp.split(m, 6, axis=-1)]
    h = rmsnorm(x, g_attn) * (1.0 + mods[1]) + mods[0]
    offs = []
    acc = 0
    for s in IN_SIZES[:-1]:
        acc += s
        offs.append(acc)
    parts = jnp.split(h @ w_in, offs, axis=-1)
    sq, sk, sv = [u.reshape(b, t, N_HEADS_SB, HEAD_DIM) for u in parts[:3]]
    nq = rope(parts[3].reshape(b, t, N_HEADS_NSA, HEAD_DIM), pos)
    ck, cv, lk, lv, wk, wv = [u.reshape(b, t, NSA_KV_HEADS, HEAD_DIM) for u in parts[4:10]]
    lk = rope(lk, pos)
    wk = rope(wk, pos)
    gates = jax.nn.sigmoid(parts[10].reshape(b, t, N_HEADS_NSA, 3))
    return mods, (sq, sk, sv, nq, ck, cv, lk, lv, wk, wv, gates)


def mix_out(x, o_sb, o_nsa, mods, lw):
    (w_ada, b_ada, g_attn, w_in, g_sb_out, g_nsa_out, w_out, pe_k, w_ck1, w_ck2, pe_v, w_cv1, w_cv2,
     g_ffn, w_router, b_router, w_e1, b_e1, w_e2, b_e2) = lw
    b, t, d = x.shape
    o = jnp.concatenate([rmsnorm(o_sb, g_sb_out), rmsnorm(o_nsa, g_nsa_out)], axis=-1)
    x = x + mods[2] * (o @ w_out)
    h = rmsnorm(x, g_ffn) * (1.0 + mods[4]) + mods[3]
    y = moe(h.reshape(b * t, d), w_router, b_router, w_e1, b_e1, w_e2, b_e2).reshape(b, t, d)
    return x + mods[5] * y


def layer_prompt(x, c, lw):
    pe_k, w_ck1, w_ck2, pe_v, w_cv1, w_cv2 = lw[7:13]
    t = x.shape[1]
    pos = jnp.arange(t)
    mods, (sq, sk, sv, nq, ck, cv, lk, lv, wk, wv, gates) = mix_in(x, c, pos, lw)
    o_sb = sb_prompt(sq, sk, sv)
    kc, vc, c_end = compress_kv(ck, cv, pe_k, w_ck1, w_ck2, pe_v, w_cv1, w_cv2)
    o_nsa = nsa_prompt(nq, gates, kc, vc, c_end, lk, lv, wk, wv)
    y = mix_out(x, o_sb, o_nsa, mods, lw)
    keep = min(WINDOW, t)
    return y, (jnp.stack([sk, sv], axis=2), jnp.stack([ck, cv, lk, lv], axis=2),
               jnp.stack([wk, wv], axis=2)[:, t - keep:])


def gather_pages(pool, layer, page_table, slot):
    rows = pool[layer, page_table, :, slot]
    return rows.reshape(rows.shape[0], rows.shape[1] * rows.shape[2], rows.shape[3], rows.shape[4])


def layer_sample(x, c, sb_pool, nsa_pool, win_buf, page_table, layer, lw):
    pe_k, w_ck1, w_ck2, pe_v, w_cv1, w_cv2 = lw[7:13]
    t = x.shape[1]
    past_len = page_table.shape[1] * sb_pool.shape[2]
    q_pos = past_len + jnp.arange(t)
    past_pos = jnp.arange(past_len)
    mods, (sq, sk, sv, nq, ck, cv, lk, lv, wk, wv, gates) = mix_in(x, c, q_pos, lw)
    o_sb = sb_attend(sq, q_pos, [(gather_pages(sb_pool, layer, page_table, 0), gather_pages(sb_pool, layer, page_table, 1), past_pos),
                                 (sk, sv, q_pos)]).reshape(x.shape[0], t, SB_WIDTH)
    ck_all = jnp.concatenate([gather_pages(nsa_pool, layer, page_table, 0), ck], axis=1)
    cv_all = jnp.concatenate([gather_pages(nsa_pool, layer, page_table, 1), cv], axis=1)
    kc, vc, c_end = compress_kv(ck_all, cv_all, pe_k, w_ck1, w_ck2, pe_v, w_cv1, w_cv2)
    ks_all = jnp.concatenate([gather_pages(nsa_pool, layer, page_table, 2), lk], axis=1)
    vs_all = jnp.concatenate([gather_pages(nsa_pool, layer, page_table, 3), lv], axis=1)
    wl = win_buf.shape[1]
    kw = jnp.concatenate([win_buf[:, :, 0], wk], axis=1)
    vw = jnp.concatenate([win_buf[:, :, 1], wv], axis=1)
    w_pos = jnp.concatenate([past_len - wl + jnp.arange(wl), q_pos])
    o_nsa = nsa_attend(nq, q_pos, kc, vc, c_end, ks_all, vs_all, kw, vw, w_pos, gates)
    y = mix_out(x, o_sb, o_nsa, mods, lw)
    return y, (jnp.stack([sk, sv], axis=2), jnp.stack([ck, cv, lk, lv], axis=2),
               jnp.stack([kw, vw], axis=2)[:, t:])


def setup_inputs(seed: int = 0) -> dict:
    key = jax.random.key(seed)
    ks = jax.random.split(key, 32)
    n_pages = PAST_LEN // PAGE_SIZE
    n_used = DEC_BATCH * n_pages
    n_pool = n_used + max(1, n_used // 4)
    win_len = min(WINDOW, PAST_LEN)
    L = DEPTH
    D = D_MODEL

    def nrm(k, shape, s):
        return s * jax.random.normal(k, shape, jnp.float32)

    page_table = jax.random.permutation(ks[7], n_pool)[:n_used].reshape(DEC_BATCH, n_pages).astype(jnp.int32)
    return {
        'x_prompt': nrm(ks[0], (BATCH, SEQ, D), 1.0),
        'x_sample': nrm(ks[1], (DEC_BATCH, DEC_SEQ, D), 1.0),
        'c_prompt': nrm(ks[2], (BATCH, D), 1.0),
        'c_sample': nrm(ks[3], (DEC_BATCH, D), 1.0),
        'cache_sb_kv': nrm(ks[4], (L, n_pool, PAGE_SIZE, 2, N_HEADS_SB, HEAD_DIM), 1.0),
        'cache_nsa_kv': nrm(ks[5], (L, n_pool, PAGE_SIZE, 4, NSA_KV_HEADS, HEAD_DIM), 1.0),
        'state_win_kv': nrm(ks[6], (L, DEC_BATCH, win_len, 2, NSA_KV_HEADS, HEAD_DIM), 1.0),
        'page_table': page_table,
        'w_ada': nrm(ks[8], (L, D, 6 * D), 0.5 * D ** -0.5),
        'b_ada': nrm(ks[9], (L, 6 * D), 0.01),
        'g_attn': 1.0 + nrm(ks[10], (L, D), 0.01),
        'w_in': nrm(ks[11], (L, D, IN_WIDTH), D ** -0.5),
        'g_sb_out': 1.0 + nrm(ks[12], (L, SB_WIDTH), 0.01),
        'g_nsa_out': 1.0 + nrm(ks[13], (L, NSA_WIDTH), 0.01),
        'w_out': nrm(ks[14], (L, MIX_WIDTH, D), MIX_WIDTH ** -0.5),
        'pe_k': nrm(ks[15], (L, CMP_BLOCK, HEAD_DIM), 0.1),
        'w_ck1': nrm(ks[16], (L, CMP_BLOCK * HEAD_DIM, CMP_HIDDEN), (CMP_BLOCK * HEAD_DIM) ** -0.5),
        'w_ck2': nrm(ks[17], (L, CMP_HIDDEN, HEAD_DIM), CMP_HIDDEN ** -0.5),
        'pe_v': nrm(ks[18], (L, CMP_BLOCK, HEAD_DIM), 0.1),
        'w_cv1': nrm(ks[19], (L, CMP_BLOCK * HEAD_DIM, CMP_HIDDEN), (CMP_BLOCK * HEAD_DIM) ** -0.5),
        'w_cv2': nrm(ks[20], (L, CMP_HIDDEN, HEAD_DIM), CMP_HIDDEN ** -0.5),
        'g_ffn': 1.0 + nrm(ks[21], (L, D), 0.01),
        'w_router': nrm(ks[22], (L, D, N_EXPERTS), D ** -0.5),
        'b_router': nrm(ks[23], (L, N_EXPERTS), 0.01),
        'w_e1': nrm(ks[24], (L, N_EXPERTS, D, 2 * D_FF), D ** -0.5),
        'b_e1': nrm(ks[25], (L, N_EXPERTS, 2 * D_FF), 0.01),
        'w_e2': nrm(ks[26], (L, N_EXPERTS, D_FF, D), D_FF ** -0.5),
        'b_e2': nrm(ks[27], (L, N_EXPERTS, D), 0.01),
        'g_final': 1.0 + nrm(ks[28], (D,), 0.01),
    }


def reference(x_prompt, x_sample, c_prompt, c_sample, cache_sb_kv, cache_nsa_kv, state_win_kv, page_table,
              w_ada, b_ada, g_attn, w_in, g_sb_out, g_nsa_out, w_out, pe_k, w_ck1, w_ck2, pe_v, w_cv1, w_cv2,
              g_ffn, w_router, b_router, w_e1, b_e1, w_e2, b_e2, g_final):
    hp, hs = x_prompt, x_sample
    p_states, s_states = [], []
    for l in range(DEPTH):
        lw = tuple(w[l] for w in (w_ada, b_ada, g_attn, w_in, g_sb_out, g_nsa_out, w_out, pe_k, w_ck1, w_ck2,
                                  pe_v, w_cv1, w_cv2, g_ffn, w_router, b_router, w_e1, b_e1, w_e2, b_e2))
        hp, sp = layer_prompt(hp, c_prompt, lw)
        hs, ss = layer_sample(hs, c_sample, cache_sb_kv, cache_nsa_kv, state_win_kv[l], page_table, l, lw)
        p_states.append(sp)
        s_states.append(ss)
    y_prompt = rmsnorm(hp, g_final)
    y_sample = rmsnorm(hs, g_final)
    sb_kv_prompt = jnp.stack([s[0] for s in p_states])
    nsa_kv_prompt = jnp.stack([s[1] for s in p_states])
    win_kv_prompt = jnp.stack([s[2] for s in p_states])
    sb_kv_sample = jnp.stack([s[0] for s in s_states])
    nsa_kv_sample = jnp.stack([s[1] for s in s_states])
    win_kv_sample = jnp.stack([s[2] for s in s_states])
    return (y_prompt, y_sample, sb_kv_prompt, nsa_kv_prompt, win_kv_prompt, sb_kv_sample, nsa_kv_sample, win_kv_sample)
```

```python
import functools

import jax
import jax.numpy as jnp
from jax import lax
from jax.experimental import pallas as pl
from jax.experimental.pallas import tpu as pltpu

HEAD_DIM = 64
N_HEADS_SB = 8
N_HEADS_NSA = 8
NSA_KV_HEADS = 2
NSA_GROUP = N_HEADS_NSA // NSA_KV_HEADS
SB_WIDTH = N_HEADS_SB * HEAD_DIM
NSA_WIDTH = N_HEADS_NSA * HEAD_DIM
KV_WIDTH = NSA_KV_HEADS * HEAD_DIM
N_GATES = 3 * N_HEADS_NSA
CMP_BLOCK = 32
CMP_STRIDE = 16
SLC_BLOCK = 64
SLC_TOP_N = 16
WINDOW = 512
N_EXPERTS = 32
TOP_K = 4
SWIGLU_LIMIT = 7.0
SWIGLU_ALPHA = 1.702
MOE_ROW_BLOCK = 256
ROPE_THETA = 10000.0
EPS = 1e-6
NEG = -1e30
FORCE_BONUS = 1e3
MASK_BIAS = -(2.0 ** 30)

LANES = 128
VMEM_LIMIT = 56 * 1024 * 1024

BF16 = jnp.bfloat16
F32 = jnp.float32


def _cparams(sem):
    return pltpu.CompilerParams(dimension_semantics=sem, vmem_limit_bytes=VMEM_LIMIT)


def _dot(a, b):
    return jnp.dot(a, b, preferred_element_type=F32)


def _dot_nt(a, b):
    return lax.dot_general(a, b, (((1,), (1,)), ((), ())), preferred_element_type=F32)


def _split2(x):
    hi = x.astype(BF16)
    lo = (x - hi.astype(F32)).astype(BF16)
    return hi, lo


def _rms(x, g):
    return (x * lax.rsqrt(jnp.mean(x * x, axis=-1, keepdims=True) + EPS)) * g


def _rope_slab(x, cos, sin_signed):
    lane = lax.broadcasted_iota(jnp.int32, x.shape, 1)
    first = (lane % HEAD_DIM) < (HEAD_DIM // 2)
    partner = jnp.where(first, pltpu.roll(x, LANES - HEAD_DIM // 2, 1), pltpu.roll(x, HEAD_DIM // 2, 1))
    return x * cos + partner * sin_signed


def _adaln_kernel(c_ref, w_ref, b_ref, o_ref):
    c = c_ref[...]
    s = (c * jax.nn.sigmoid(c)).astype(BF16)
    o_ref[...] = _dot(s, w_ref[...]) + b_ref[...]


def _adaln(c, w_ada_b, b_ada):
    r, d = c.shape
    n = w_ada_b.shape[1]
    tn = d
    return pl.pallas_call(
        _adaln_kernel,
        grid=(n // tn,),
        in_specs=[pl.BlockSpec((r, d), lambda j: (0, 0)),
                  pl.BlockSpec((d, tn), lambda j: (0, j)),
                  pl.BlockSpec((1, tn), lambda j: (0, j))],
        out_specs=pl.BlockSpec((r, tn), lambda j: (0, j)),
        out_shape=jax.ShapeDtypeStruct((r, n), F32),
        compiler_params=_cparams(("arbitrary",)),
        name="adaln",
    )(c, w_ada_b, b_ada.reshape(1, n))


_C_SQ = 0
_C_SBKV = SB_WIDTH
_C_NQ = 3 * SB_WIDTH
_C_NSAKV = _C_NQ + NSA_WIDTH
_C_WINKV = _C_NSAKV + 4 * KV_WIDTH
_C_DUP = _C_WINKV + 2 * KV_WIDTH
_C_GATE = _C_DUP + 8 * LANES
_C_END = _C_GATE + LANES


def _prep_w_in(w_in):
    main = w_in[:, :_C_DUP]
    dups = []
    for g in range(NSA_KV_HEADS):
        for slot in (2, 3, 4, 5):
            c0 = _C_NSAKV + slot * KV_WIDTH + g * HEAD_DIM
            col = w_in[:, c0:c0 + HEAD_DIM]
            dups += [col, col]
    gates = jnp.pad(w_in[:, _C_DUP:_C_DUP + N_GATES], ((0, 0), (0, LANES - N_GATES)))
    return jnp.concatenate([main] + dups + [gates], axis=1).astype(BF16)


def _inproj_kernel(x_ref, shift_ref, scale_ref, g_ref, w_ref, cos_ref, sin_ref,
                   sq_ref, sbkv_ref, sbkvb_ref, nq_ref, nsakv_ref, winkv_ref, dup_ref, gate_ref):
    x = x_ref[0]
    h = _rms(x, g_ref[...]) * (1.0 + scale_ref[0]) + shift_ref[0]
    hb = h.astype(BF16)
    cos = cos_ref[...]
    sin = sin_ref[...]
    qscale = HEAD_DIM ** -0.5

    def proj(c0, width):
        return _dot(hb, w_ref[:, c0:c0 + width])

    sq_ref[0] = (proj(_C_SQ, SB_WIDTH) * qscale).astype(BF16)
    kv = proj(_C_SBKV, 2 * SB_WIDTH)
    sbkv_ref[0] = kv
    sbkvb_ref[0] = kv.astype(BF16)
    for p in range(NSA_WIDTH // LANES):
        y = _rope_slab(proj(_C_NQ + p * LANES, LANES), cos, sin)
        nq_ref[0, :, p * LANES:(p + 1) * LANES] = (y * qscale).astype(BF16)
    for s in range(4):
        y = proj(_C_NSAKV + s * LANES, LANES)
        if s == 2:
            y = _rope_slab(y, cos, sin)
        nsakv_ref[0, :, s * LANES:(s + 1) * LANES] = y
    for s in range(2):
        y = proj(_C_WINKV + s * LANES, LANES)
        if s == 0:
            y = _rope_slab(y, cos, sin)
        winkv_ref[0, :, s * LANES:(s + 1) * LANES] = y
    for s in range(8):
        y = proj(_C_DUP + s * LANES, LANES)
        if s % 2 == 0:
            y = _rope_slab(y, cos, sin)
        dup_ref[0, :, s * LANES:(s + 1) * LANES] = y.astype(BF16)
    gate_ref[0] = jax.nn.sigmoid(proj(_C_GATE, LANES))


def _rope_tables(pos):
    half = HEAD_DIM // 2
    inv_freq = jnp.power(ROPE_THETA, -jnp.arange(half, dtype=F32) / half)
    ang = pos.astype(F32)[:, None] * inv_freq[None, :]
    cos, sin = jnp.cos(ang), jnp.sin(ang)
    cos_t = jnp.concatenate([cos, cos, cos, cos], axis=1)
    sin_t = jnp.concatenate([-sin, sin, -sin, sin], axis=1)
    return cos_t, sin_t


def _inproj(x, shift, scale, g_attn, w_all, pos):
    b, t, d = x.shape
    tm = min(256, t)
    tmod = shift.shape[1]
    cos_t, sin_t = _rope_tables(pos)
    if tmod == 1:
        mod_spec = pl.BlockSpec((1, 1, d), lambda i, j: (i, 0, 0))
    else:
        mod_spec = pl.BlockSpec((1, tm, d), lambda i, j: (i, j, 0))
    widths = (SB_WIDTH, 2 * SB_WIDTH, 2 * SB_WIDTH, NSA_WIDTH, 4 * KV_WIDTH, 2 * KV_WIDTH, 8 * LANES, LANES)
    dtypes = (BF16, F32, BF16, BF16, F32, F32, BF16, F32)
    return pl.pallas_call(
        _inproj_kernel,
        grid=(b, t // tm),
        in_specs=[pl.BlockSpec((1, tm, d), lambda i, j: (i, j, 0)),
                  mod_spec, mod_spec,
                  pl.BlockSpec((1, d), lambda i, j: (0, 0)),
                  pl.BlockSpec((d, _C_END), lambda i, j: (0, 0)),
                  pl.BlockSpec((tm, LANES), lambda i, j: (j, 0)),
                  pl.BlockSpec((tm, LANES), lambda i, j: (j, 0))],
        out_specs=[pl.BlockSpec((1, tm, w), lambda i, j: (i, j, 0)) for w in widths],
        out_shape=[jax.ShapeDtypeStruct((b, t, w), dt) for w, dt in zip(widths, dtypes)],
        compiler_params=_cparams(("arbitrary", "arbitrary")),
        name="inproj",
    )(x, shift, scale, g_attn.reshape(1, d), w_all, cos_t, sin_t)


def _sb_tile(q_heads, k, v_heads, tri, carry, diag_mask):
    rs = list(carry[:2])
    acc = carry[2]
    for hh in range(2):
        z = _dot_nt(q_heads[hh], k)
        t = jnp.log(1.0 + jnp.exp(-jnp.abs(z)))
        log_keep = -jnp.maximum(z, 0.0) - t
        log_beta = jnp.minimum(z, 0.0) - t
        if diag_mask is not None:
            log_keep = jnp.where(diag_mask, log_keep, 0.0)
        hi, lo = _split2(log_keep)
        stick = _dot(hi, tri) + _dot(lo, tri) + rs[hh]
        a = jnp.exp(log_beta + stick)
        if diag_mask is not None:
            a = jnp.where(diag_mask, a, 0.0)
        acc = acc + _dot(a.astype(BF16), v_heads[hh])
        rs[hh] = rs[hh] + jnp.sum(log_keep, axis=-1, keepdims=True)
    return rs[0], rs[1], acc


def _sb_kernel(q_ref, k_ref, v_ref, o_ref, *, tq):
    qi = pl.program_id(2)
    q = q_ref[0]
    lane = lax.broadcasted_iota(jnp.int32, (1, LANES), 1)
    head_lanes = [lane < HEAD_DIM, lane >= HEAD_DIM]
    zero = jnp.zeros((), BF16)
    q_heads = [jnp.where(m, q, zero) for m in head_lanes]
    row = lax.broadcasted_iota(jnp.int32, (tq, tq), 0)
    col = lax.broadcasted_iota(jnp.int32, (tq, tq), 1)
    tri = (row > col).astype(BF16)
    diag_mask = col < row

    def tile(j, carry, mask):
        start = pl.multiple_of(j * tq, tq)
        k = k_ref[0, pl.ds(start, tq), :]
        v = v_ref[0, pl.ds(start, tq), :]
        v_heads = [jnp.where(m, v, zero) for m in head_lanes]
        return _sb_tile(q_heads, k, v_heads, tri, carry, mask)

    init = (jnp.zeros((tq, 1), F32), jnp.zeros((tq, 1), F32), jnp.zeros((tq, LANES), F32))
    carry = tile(qi, init, diag_mask)
    carry = lax.fori_loop(0, qi, lambda s, c: tile(qi - 1 - s, c, None), carry)
    o_ref[0] = carry[2]


def _sb_prompt(sq_b, sbkv_b):
    b, t, _ = sq_b.shape
    tq = min(256, t)
    n_pair = SB_WIDTH // LANES
    return pl.pallas_call(
        functools.partial(_sb_kernel, tq=tq),
        grid=(b, n_pair, t // tq),
        in_specs=[pl.BlockSpec((1, tq, LANES), lambda i, p, j: (i, j, p)),
                  pl.BlockSpec((1, t, LANES), lambda i, p, j: (i, 0, p)),
                  pl.BlockSpec((1, t, LANES), lambda i, p, j: (i, 0, n_pair + p))],
        out_specs=pl.BlockSpec((1, tq, LANES), lambda i, p, j: (i, j, p)),
        out_shape=jax.ShapeDtypeStruct((b, t, SB_WIDTH), F32),
        compiler_params=_cparams(("arbitrary", "arbitrary", "arbitrary")),
        name="sb_prompt",
    )(sq_b, sbkv_b, sbkv_b)


def _prep_cmp_weights(pe, w1, w2):
    hid = w1.shape[1]
    w1r = w1.reshape(CMP_BLOCK, HEAD_DIM, hid)
    z = jnp.zeros_like(w1r)
    w1_bd = jnp.concatenate([jnp.concatenate([w1r, z], axis=2), jnp.concatenate([z, w1r], axis=2)], axis=1)
    w2d = jnp.concatenate([w2, w2], axis=1)
    z2 = jnp.zeros_like(w2d)
    w2_bd = jnp.concatenate([jnp.concatenate([w2d, z2], axis=1), jnp.concatenate([z2, w2d], axis=1)], axis=0)
    pe_d = jnp.concatenate([pe, pe], axis=1)
    return pe_d, w1_bd.astype(BF16), w2_bd.astype(BF16)


def _compress_kernel(x_ref, pe_ref, w1_ref, w2_ref, cos_ref, sin_ref, o_ref, bot_ref, *, rope):
    n = x_ref.shape[1]
    hid2 = w1_ref.shape[2]
    top = jnp.zeros((n, hid2), F32)
    bot = jnp.zeros((n, hid2), F32)
    for r in range(CMP_STRIDE):
        xr = x_ref[0, :, r, :]
        top = top + _dot((xr + pe_ref[r:r + 1, :]).astype(BF16), w1_ref[r])
        bot = bot + _dot((xr + pe_ref[CMP_STRIDE + r:CMP_STRIDE + r + 1, :]).astype(BF16), w1_ref[CMP_STRIDE + r])
    bot_ref[pl.ds(0, n), :] = bot
    bot_ref[pl.ds(n, 8), :] = jnp.zeros((8, hid2), F32)
    hidden = top + bot_ref[pl.ds(1, n), :]
    hidden = hidden * jax.nn.sigmoid(hidden)
    out = _dot(hidden.astype(BF16), w2_ref[...])
    for g in range(NSA_KV_HEADS):
        y = out[:, g * LANES:(g + 1) * LANES]
        if rope:
            y = _rope_slab(y, cos_ref[...], sin_ref[...])
        o_ref[0, g] = y.astype(BF16)


def _compress(rows4, slot, pe, w1, w2, rope):
    b, n, _, _ = rows4.shape
    pe_d, w1_bd, w2_bd = _prep_cmp_weights(pe, w1, w2)
    c_end = jnp.arange(n) * CMP_STRIDE + CMP_BLOCK - 1
    cos_t, sin_t = _rope_tables(c_end)
    hid2 = w1_bd.shape[2]
    return pl.pallas_call(
        functools.partial(_compress_kernel, rope=rope),
        grid=(b,),
        in_specs=[pl.BlockSpec((1, n, CMP_STRIDE, LANES), lambda i: (i, 0, 0, slot)),
                  pl.BlockSpec((CMP_BLOCK, LANES), lambda i: (0, 0)),
                  pl.BlockSpec((CMP_BLOCK, LANES, hid2), lambda i: (0, 0, 0)),
                  pl.BlockSpec((hid2, 2 * LANES), lambda i: (0, 0)),
                  pl.BlockSpec((n, LANES), lambda i: (0, 0)),
                  pl.BlockSpec((n, LANES), lambda i: (0, 0))],
        out_specs=pl.BlockSpec((1, NSA_KV_HEADS, n, LANES), lambda i: (i, 0, 0, 0)),
        out_shape=jax.ShapeDtypeStruct((b, NSA_KV_HEADS, n, LANES), BF16),
        scratch_shapes=[pltpu.VMEM((n + 8, hid2), F32)],
        compiler_params=_cparams(("arbitrary",)),
        name="compress",
    )(rows4, pe_d, w1_bd, w2_bd, cos_t, sin_t)


def _nsa_kernel(q_ref, gate_ref, kc_ref, vc_ref, ovl_ref, ks_ref, vs_ref, kw_ref, vw_ref, o_ref, *, tq, tk, n_top):
    g = pl.program_id(1)
    qi = pl.program_id(2)
    t_len = ks_ref.shape[2]
    n_c = kc_ref.shape[2]
    n_slc = ovl_ref.shape[0]
    s0 = qi * tq
    zero = jnp.zeros((), BF16)
    lane = lax.broadcasted_iota(jnp.int32, (1, LANES), 1)
    head_lanes = [lane < HEAD_DIM, lane >= HEAD_DIM]
    q = q_ref[0]
    q_rows = jnp.concatenate(
        [jnp.where(head_lanes[h % 2], q[:, (h // 2) * LANES:(h // 2 + 1) * LANES], zero) for h in range(NSA_GROUP)],
        axis=0)
    q_pos = s0 + lax.broadcasted_iota(jnp.int32, (tq, 1), 0)
    q_pos4 = jnp.concatenate([q_pos] * NSA_GROUP, axis=0)

    def head_merge(o_rows):
        slabs = []
        for p in range(NSA_GROUP // 2):
            a = o_rows[(2 * p) * tq:(2 * p + 1) * tq]
            b2 = o_rows[(2 * p + 1) * tq:(2 * p + 2) * tq]
            slabs.append(jnp.where(head_lanes[0], a, b2))
        return jnp.concatenate(slabs, axis=1)

    kc = kc_ref[0, 0]
    vc = vc_ref[0, 0]
    s = _dot_nt(q_rows, kc)
    c_idx = lax.broadcasted_iota(jnp.int32, (1, n_c), 1)
    c_end = c_idx * CMP_STRIDE + (CMP_BLOCK - 1)
    c_mask = (c_end <= q_pos4) & (c_idx < n_c - 1)
    s = jnp.where(c_mask, s, NEG)
    m = jnp.max(s, axis=-1, keepdims=True)
    e = jnp.where(c_mask, jnp.exp(s - m), 0.0)
    p_cmp = e / jnp.maximum(jnp.sum(e, axis=-1, keepdims=True), 1.0)
    o_cmp = head_merge(_dot(p_cmp.astype(BF16), vc))

    p_sum = p_cmp[0:tq]
    for h in range(1, NSA_GROUP):
        p_sum = p_sum + p_cmp[h * tq:(h + 1) * tq]
    ovl = ovl_ref[...]
    p_hi, p_lo = _split2(p_sum)
    p_lo2 = (p_sum - p_hi.astype(F32) - p_lo.astype(F32)).astype(BF16)
    imp = _dot_nt(ovl, p_hi) + _dot_nt(ovl, p_lo) + _dot_nt(ovl, p_lo2)
    blk = lax.broadcasted_iota(jnp.int32, (n_slc, 1), 0)
    qp_l = s0 + lax.broadcasted_iota(jnp.int32, (1, tq), 1)
    cur = qp_l // SLC_BLOCK
    forced = (blk == 0) | (blk == cur) | (blk == cur - 1)
    imp = jnp.where(blk * SLC_BLOCK <= qp_l, imp + jnp.where(forced, FORCE_BONUS, 0.0), NEG)
    blk_full = lax.broadcasted_iota(jnp.int32, (n_slc, tq), 0)

    def pick(_, carry):
        imp_c, unsel = carry
        best = jnp.max(imp_c, axis=0, keepdims=True)
        first = jnp.min(jnp.where(imp_c == best, blk_full, n_slc), axis=0, keepdims=True)
        hit = blk_full == first
        return jnp.where(hit, -jnp.inf, imp_c), jnp.where(hit, 0.0, unsel)

    _, unsel_t = lax.fori_loop(0, n_top, pick, (imp, jnp.ones((n_slc, tq), F32)))
    unsel = unsel_t.T.astype(BF16)
    q_aug = jnp.concatenate([q_rows, jnp.concatenate([unsel] * NSA_GROUP, axis=0)], axis=1)

    blocks_per_tile = tk // SLC_BLOCK
    k_row_blk = lax.broadcasted_iota(jnp.int32, (tk, n_slc), 0) // SLC_BLOCK
    k_lane = lax.broadcasted_iota(jnp.int32, (tk, n_slc), 1)
    k_off = lax.broadcasted_iota(jnp.int32, (1, tk), 1)

    def slc_tile(j, carry):
        m_run, l_run, acc = carry
        start = pl.multiple_of(j * tk, tk)
        k = ks_ref[0, 0, pl.ds(start, tk), :]
        v = vs_ref[0, 0, pl.ds(start, tk), :]
        onehot = jnp.where(k_lane == k_row_blk + j * blocks_per_tile, MASK_BIAS, 0.0).astype(BF16)
        k_aug = jnp.concatenate([k, onehot], axis=1)
        sc = _dot_nt(q_aug, k_aug)
        sc = jnp.where(start + k_off <= q_pos4, sc, MASK_BIAS)
        m_new = jnp.maximum(m_run, jnp.max(sc, axis=-1, keepdims=True))
        alpha = jnp.exp(m_run - m_new)
        pr = jnp.exp(sc - m_new)
        l_new = alpha * l_run + jnp.sum(pr, axis=-1, keepdims=True)
        acc = alpha * acc + _dot(pr.astype(BF16), v)
        return m_new, l_new, acc

    n_tiles = (s0 + tq + tk - 1) // tk
    init = (jnp.full((NSA_GROUP * tq, 1), NEG, F32), jnp.zeros((NSA_GROUP * tq, 1), F32),
            jnp.zeros((NSA_GROUP * tq, LANES), F32))
    _, l_fin, acc = lax.fori_loop(0, n_tiles, slc_tile, init)
    o_slc = head_merge(acc / jnp.maximum(l_fin, 1.0))

    band = min(WINDOW + tq, t_len)
    w_start = jnp.clip(s0 - WINDOW, 0, t_len - band)
    w_start = pl.multiple_of(w_start, tq)
    kw = kw_ref[0, 0, pl.ds(w_start, band), :]
    vw = vw_ref[0, 0, pl.ds(w_start, band), :]
    sw = _dot_nt(q_rows, kw)
    w_pos = w_start + lax.broadcasted_iota(jnp.int32, (1, band), 1)
    dist = q_pos4 - w_pos
    w_mask = (dist >= 0) & (dist < WINDOW)
    sw = jnp.where(w_mask, sw, NEG)
    mw = jnp.max(sw, axis=-1, keepdims=True)
    ew = jnp.where(w_mask, jnp.exp(sw - mw), 0.0)
    pw = ew / jnp.maximum(jnp.sum(ew, axis=-1, keepdims=True), 1.0)
    o_win = head_merge(_dot(pw.astype(BF16), vw))

    gates = gate_ref[0]
    g_hi, g_lo = _split2(gates)
    col = lax.broadcasted_iota(jnp.int32, (LANES, NSA_GROUP * HEAD_DIM), 0)
    head_of_lane = lax.broadcasted_iota(jnp.int32, (LANES, NSA_GROUP * HEAD_DIM), 1) // HEAD_DIM + g * NSA_GROUP
    out = jnp.zeros((tq, NSA_GROUP * HEAD_DIM), F32)
    for branch, o_b in enumerate((o_cmp, o_slc, o_win)):
        expand = (col == head_of_lane * 3 + branch).astype(BF16)
        out = out + (_dot(g_hi, expand) + _dot(g_lo, expand)) * o_b
    o_ref[0] = out


def _cmp_overlap_t(n_c, n_slc):
    cs = jnp.arange(n_c) * CMP_STRIDE
    ss = jnp.arange(n_slc) * SLC_BLOCK
    ok = (cs[None, :] < ss[:, None] + SLC_BLOCK) & (cs[None, :] + CMP_BLOCK > ss[:, None]) & (jnp.arange(n_c)[None, :] < n_c - 1)
    return ok.astype(BF16)


def _nsa_prompt(nq_b, gates, kc, vc, dup_b):
    b, t, _ = nq_b.shape
    n_c = kc.shape[2]
    n_slc = t // SLC_BLOCK
    tq = 128
    tk = min(512, t)
    n_top = min(SLC_TOP_N, n_slc)
    gw = NSA_GROUP * HEAD_DIM
    kv_spec = lambda slot: pl.BlockSpec((1, 1, t, LANES), lambda i, g, j: (i, 0, 0, g * 4 + slot))
    dup4 = dup_b.reshape(b, 1, t, 8 * LANES)
    return pl.pallas_call(
        functools.partial(_nsa_kernel, tq=tq, tk=tk, n_top=n_top),
        grid=(b, NSA_KV_HEADS, t // tq),
        in_specs=[pl.BlockSpec((1, tq, gw), lambda i, g, j: (i, j, g)),
                  pl.BlockSpec((1, tq, LANES), lambda i, g, j: (i, j, 0)),
                  pl.BlockSpec((1, 1, n_c, LANES), lambda i, g, j: (i, g, 0, 0)),
                  pl.BlockSpec((1, 1, n_c, LANES), lambda i, g, j: (i, g, 0, 0)),
                  pl.BlockSpec((n_slc, n_c), lambda i, g, j: (0, 0)),
                  kv_spec(0), kv_spec(1), kv_spec(2), kv_spec(3)],
        out_specs=pl.BlockSpec((1, tq, gw), lambda i, g, j: (i, j, g)),
        out_shape=jax.ShapeDtypeStruct((b, t, NSA_WIDTH), F32),
        compiler_params=_cparams(("arbitrary", "arbitrary", "arbitrary")),
        name="nsa_prompt",
    )(nq_b, gates, kc, vc, _cmp_overlap_t(n_c, n_slc), dup4, dup4, dup4, dup4)


def _outproj_kernel(x_ref, osb_ref, onsa_ref, gate_ref, shift_ref, scale_ref, gsb_ref, gnsa_ref, gffn_ref,
                    wo_ref, wr_ref, br_ref, x1_ref, h_ref, logit_ref):
    o_sb = _rms(osb_ref[0], gsb_ref[...]).astype(BF16)
    o_nsa = _rms(onsa_ref[0], gnsa_ref[...]).astype(BF16)
    mixed = _dot(o_sb, wo_ref[0:SB_WIDTH, :]) + _dot(o_nsa, wo_ref[SB_WIDTH:, :])
    x1 = x_ref[0] + gate_ref[0] * mixed
    x1_ref[0] = x1
    h = _rms(x1, gffn_ref[...]) * (1.0 + scale_ref[0]) + shift_ref[0]
    h_ref[0] = h.astype(BF16)
    h_hi, h_lo = _split2(h)
    logit_ref[0] = _dot(h_hi, wr_ref[0]) + _dot(h_lo, wr_ref[0]) + _dot(h_hi, wr_ref[1]) + br_ref[...]


def _outproj(x, o_sb, o_nsa, gate, shift, scale, g_sb, g_nsa, g_ffn, w_out_b, w_router, b_router):
    b, t, d = x.shape
    tm = min(256, t)
    tmod = gate.shape[1]
    if tmod == 1:
        mod_spec = pl.BlockSpec((1, 1, d), lambda i, j: (i, 0, 0))
    else:
        mod_spec = pl.BlockSpec((1, tm, d), lambda i, j: (i, j, 0))
    wr = jnp.pad(w_router, ((0, 0), (0, LANES - N_EXPERTS)))
    wr_hi = wr.astype(BF16)
    wr_lo = (wr - wr_hi.astype(F32)).astype(BF16)
    wr2 = jnp.stack([wr_hi, wr_lo])
    br = jnp.pad(b_router, (0, LANES - N_EXPERTS)).reshape(1, LANES)
    row = lambda w: pl.BlockSpec((1, tm, w), lambda i, j: (i, j, 0))
    vec = lambda w: pl.BlockSpec((1, w), lambda i, j: (0, 0))
    return pl.pallas_call(
        _outproj_kernel,
        grid=(b, t // tm),
        in_specs=[row(d), row(SB_WIDTH), row(NSA_WIDTH), mod_spec, mod_spec, mod_spec,
                  vec(SB_WIDTH), vec(NSA_WIDTH), vec(d),
                  pl.BlockSpec((SB_WIDTH + NSA_WIDTH, d), lambda i, j: (0, 0)),
                  pl.BlockSpec((2, d, LANES), lambda i, j: (0, 0, 0)),
                  vec(LANES)],
        out_specs=[row(d), row(d), row(LANES)],
        out_shape=[jax.ShapeDtypeStruct((b, t, d), F32), jax.ShapeDtypeStruct((b, t, d), BF16),
                   jax.ShapeDtypeStruct((b, t, LANES), F32)],
        compiler_params=_cparams(("arbitrary", "arbitrary")),
        name="outproj",
    )(x, o_sb, o_nsa, gate, shift, scale, g_sb.reshape(1, -1), g_nsa.reshape(1, -1), g_ffn.reshape(1, -1),
      w_out_b, wr2, br)


def _moe_kernel(be_ref, xs_ref, gate_ref, w1_ref, b1_ref, w2_ref, b2_ref, o_ref):
    del be_ref
    d_ff = w2_ref.shape[1]
    gu = _dot(xs_ref[...], w1_ref[0]) + b1_ref[0]
    gl = jnp.minimum(gu[:, :d_ff], SWIGLU_LIMIT)
    u = jnp.clip(gu[:, d_ff:], -SWIGLU_LIMIT, SWIGLU_LIMIT)
    act = (u + 1.0) * gl * jax.nn.sigmoid(SWIGLU_ALPHA * gl)
    y = _dot(act.astype(BF16), w2_ref[0]) + b2_ref[0]
    o_ref[...] = y * gate_ref[...]


def _moe_ffn(xs, row_gate, block_expert, w1_b, b1, w2_b, b2, rb):
    rows, d = xs.shape
    n_blocks = rows // rb
    e, _, f2 = w1_b.shape
    grid_spec = pltpu.PrefetchScalarGridSpec(
        num_scalar_prefetch=1,
        grid=(n_blocks,),
        in_specs=[pl.BlockSpec((rb, d), lambda i, be: (i, 0)),
                  pl.BlockSpec((rb, 1), lambda i, be: (i, 0)),
                  pl.BlockSpec((1, d, f2), lambda i, be: (be[i], 0, 0)),
                  pl.BlockSpec((1, 1, f2), lambda i, be: (be[i], 0, 0)),
                  pl.BlockSpec((1, f2 // 2, d), lambda i, be: (be[i], 0, 0)),
                  pl.BlockSpec((1, 1, d), lambda i, be: (be[i], 0, 0))],
        out_specs=pl.BlockSpec((rb, d), lambda i, be: (i, 0)),
    )
    return pl.pallas_call(
        _moe_kernel,
        grid_spec=grid_spec,
        out_shape=jax.ShapeDtypeStruct((rows, d), F32),
        compiler_params=_cparams(("arbitrary",)),
        name="moe_ffn",
    )(block_expert, xs, row_gate.reshape(rows, 1), w1_b, b1.reshape(e, 1, f2), w2_b, b2.reshape(e, 1, d))


def _moe(h_b, logits, w1_b, b1, w2_b, b2):
    n, d = h_b.shape
    a = n * TOP_K
    top_val, top_idx = lax.top_k(logits, TOP_K)
    gate = jax.nn.softmax(top_val, axis=-1)
    e_flat = top_idx.reshape(a)
    tok_flat = jnp.repeat(jnp.arange(n, dtype=jnp.int32), TOP_K)
    g_flat = gate.reshape(a)
    rb = MOE_ROW_BLOCK if a >= MOE_ROW_BLOCK * N_EXPERTS else max(8, a // N_EXPERTS)
    order = jnp.argsort(e_flat)
    e_sorted = e_flat[order]
    counts = jnp.bincount(e_flat, length=N_EXPERTS)
    padded = (counts + rb - 1) // rb * rb
    start = jnp.cumsum(counts) - counts
    pad_end = jnp.cumsum(padded)
    pad_start = pad_end - padded
    dest = pad_start[e_sorted] + jnp.arange(a) - start[e_sorted]
    n_blocks = -(-a // rb) + N_EXPERTS
    rows = n_blocks * rb
    row_tok = jnp.full((rows,), n, jnp.int32).at[dest].set(tok_flat[order])
    row_gate = jnp.zeros((rows,), F32).at[dest].set(g_flat[order])
    block_expert = jnp.minimum(jnp.searchsorted(pad_end, jnp.arange(n_blocks) * rb, side='right'),
                               N_EXPERTS - 1).astype(jnp.int32)
    xs = jnp.concatenate([h_b, jnp.zeros((1, d), h_b.dtype)])[row_tok]
    ys = _moe_ffn(xs, row_gate, block_expert, w1_b, b1, w2_b, b2, rb)
    return jnp.zeros((n + 1, d), F32).at[row_tok].add(ys)[:n]


def _final_kernel(x1_ref, y_ref, gate_ref, g_ref, o_ref):
    o_ref[0] = _rms(x1_ref[0] + gate_ref[0] * y_ref[0], g_ref[...])


def _final(x1, y, gate, g_final):
    b, t, d = x1.shape
    tm = min(512, t)
    tmod = gate.shape[1]
    if tmod == 1:
        mod_spec = pl.BlockSpec((1, 1, d), lambda i, j: (i, 0, 0))
    else:
        mod_spec = pl.BlockSpec((1, tm, d), lambda i, j: (i, j, 0))
    row = pl.BlockSpec((1, tm, d), lambda i, j: (i, j, 0))
    return pl.pallas_call(
        _final_kernel,
        grid=(b, t // tm),
        in_specs=[row, row, mod_spec, pl.BlockSpec((1, d), lambda i, j: (0, 0))],
        out_specs=row,
        out_shape=jax.ShapeDtypeStruct((b, t, d), F32),
        compiler_params=_cparams(("arbitrary", "arbitrary")),
        name="final",
    )(x1, y, gate, g_final.reshape(1, d))


def _rope(x, pos):
    half = HEAD_DIM // 2
    inv_freq = jnp.power(ROPE_THETA, -jnp.arange(half, dtype=F32) / half)
    ang = pos.astype(F32)[:, None] * inv_freq[None, :]
    cos = jnp.cos(ang)[None, :, None, :]
    sin = jnp.sin(ang)[None, :, None, :]
    x1, x2 = x[..., :half], x[..., half:]
    return jnp.concatenate([x1 * cos - x2 * sin, x2 * cos + x1 * sin], axis=-1)


def _masked_softmax(s, mask):
    s = jnp.where(mask, s, NEG)
    m = jnp.max(s, axis=-1, keepdims=True)
    e = jnp.where(mask, jnp.exp(s - m), 0.0)
    return e / jnp.maximum(jnp.sum(e, axis=-1, keepdims=True), 1.0)


def _sb_attend(q, q_pos, segments):
    scale = HEAD_DIM ** -0.5
    z = jnp.concatenate([jnp.einsum('bqhd,bkhd->bhqk', q, k).astype(F32) for k, _, _ in segments], axis=-1) * scale
    k_pos = jnp.concatenate([p for _, _, p in segments])
    mask = k_pos[None, :] < q_pos[:, None]
    log_keep = jnp.where(mask, jax.nn.log_sigmoid(-z), 0.0)
    log_stick = lax.cumsum(log_keep, axis=3, reverse=True) - log_keep
    a = jnp.where(mask, jnp.exp(jax.nn.log_sigmoid(z) + log_stick), 0.0)
    outs = []
    off = 0
    for k, v, _ in segments:
        n = k.shape[1]
        outs.append(jnp.einsum('bhqk,bkhd->bqhd', a[..., off:off + n], v))
        off += n
    return sum(outs[1:], outs[0])


def _compress_jax(rows, pe, w1, w2):
    b, t, g, d = rows.shape
    t_pad = -(-t // CMP_STRIDE) * CMP_STRIDE
    rows = jnp.pad(rows, ((0, 0), (0, t_pad - t), (0, 0), (0, 0)))
    n_chunk = t_pad // CMP_STRIDE
    ratio = CMP_BLOCK // CMP_STRIDE
    n_cmp = n_chunk - ratio + 1
    ch = rows.reshape(b, n_chunk, CMP_STRIDE, g, d)
    blocks = jnp.concatenate([ch[:, r:r + n_cmp] for r in range(ratio)], axis=2)
    blocks = blocks + pe[:, None, :]
    flat = blocks.transpose(0, 1, 3, 2, 4).reshape(b, n_cmp, g, CMP_BLOCK * d)
    out = jax.nn.silu(flat @ w1) @ w2
    end = jnp.arange(n_cmp) * CMP_STRIDE + CMP_BLOCK - 1
    return out, end


def _cmp_to_slc(n_cmp, n_slc):
    cs = jnp.arange(n_cmp) * CMP_STRIDE
    ss = jnp.arange(n_slc) * SLC_BLOCK
    return ((cs[:, None] < ss[None, :] + SLC_BLOCK) & (cs[:, None] + CMP_BLOCK > ss[None, :])).astype(F32)


def _nsa_attend(q, q_pos, kc, vc, c_end, ks, vs, kw, vw, w_pos, gates):
    b, tq = q.shape[:2]
    scale = HEAD_DIM ** -0.5
    qg = q.reshape(b, tq, NSA_KV_HEADS, NSA_GROUP, HEAD_DIM)
    s = jnp.einsum('bqgrd,bngd->bgrqn', qg, kc).astype(F32) * scale
    p_cmp = _masked_softmax(s, c_end[None, :] <= q_pos[:, None])
    o_cmp = jnp.einsum('bgrqn,bngd->bqgrd', p_cmp, vc)
    tk = ks.shape[1]
    n_slc = -(-tk // SLC_BLOCK)
    n_top = min(SLC_TOP_N, n_slc)
    imp = jnp.einsum('bgrqn,ns->bgqs', p_cmp, _cmp_to_slc(kc.shape[1], n_slc), precision=lax.Precision.HIGHEST)
    blk = jnp.arange(n_slc)[None, :]
    cur = (q_pos // SLC_BLOCK)[:, None]
    forced = (blk == 0) | (blk == cur) | (blk == cur - 1)
    imp = jnp.where(blk * SLC_BLOCK <= q_pos[:, None], imp + jnp.where(forced, FORCE_BONUS, 0.0), NEG)
    _, idx = lax.top_k(imp, n_top)
    pad = ((0, 0), (0, n_slc * SLC_BLOCK - tk), (0, 0), (0, 0))

    def to_blocks(t):
        return jnp.pad(t, pad).reshape(b, n_slc, SLC_BLOCK, NSA_KV_HEADS, HEAD_DIM).transpose(0, 3, 1, 2, 4)

    take = jax.vmap(jax.vmap(lambda blocks, ids: blocks[ids]))
    n_keys = n_top * SLC_BLOCK
    kg = take(to_blocks(ks), idx).reshape(b, NSA_KV_HEADS, tq, n_keys, HEAD_DIM)
    vg = take(to_blocks(vs), idx).reshape(b, NSA_KV_HEADS, tq, n_keys, HEAD_DIM)
    k_pos = (idx[..., None] * SLC_BLOCK + jnp.arange(SLC_BLOCK)).reshape(b, NSA_KV_HEADS, 1, tq, n_keys)
    s = jnp.einsum('bqgrd,bgqmd->bgrqm', qg, kg).astype(F32) * scale
    p = _masked_softmax(s, k_pos <= q_pos[:, None])
    o_slc = jnp.einsum('bgrqm,bgqmd->bqgrd', p, vg)
    s = jnp.einsum('bqgrd,bwgd->bgrqw', qg, kw).astype(F32) * scale
    dist = q_pos[:, None] - w_pos[None, :]
    p = _masked_softmax(s, (w_pos[None, :] >= 0) & (dist >= 0) & (dist < WINDOW))
    o_win = jnp.einsum('bgrqw,bwgd->bqgrd', p, vw)
    gt = gates.reshape(b, tq, NSA_KV_HEADS, NSA_GROUP, 3)
    o = gt[..., 0:1] * o_cmp + gt[..., 1:2] * o_slc + gt[..., 2:3] * o_win
    return o.reshape(b, tq, NSA_WIDTH)


def _gather_pages(pool, layer, page_table, slot):
    rows = pool[layer, page_table, :, slot]
    return rows.reshape(rows.shape[0], rows.shape[1] * rows.shape[2], rows.shape[3], rows.shape[4])


def _mix_out(x, o_sb, o_nsa, mods, lw, g_final):
    (gate_attn, shift_ffn, scale_ffn, gate_ffn) = mods
    (g_sb_out, g_nsa_out, w_out_b, g_ffn, w_router, b_router, w1_b, b_e1, w2_b, b_e2) = lw
    b, t, d = x.shape
    x1, h_b, logits = _outproj(x, o_sb, o_nsa, gate_attn, shift_ffn, scale_ffn, g_sb_out, g_nsa_out, g_ffn,
                               w_out_b, w_router, b_router)
    y = _moe(h_b.reshape(b * t, d), logits.reshape(b * t, LANES)[:, :N_EXPERTS], w1_b, b_e1, w2_b, b_e2)
    return _final(x1, y.reshape(b, t, d), gate_ffn, g_final)


def kernel(x_prompt, x_sample, c_prompt, c_sample, cache_sb_kv, cache_nsa_kv, state_win_kv, page_table, w_ada, b_ada, g_attn, w_in, g_sb_out, g_nsa_out, w_out, pe_k, w_ck1, w_ck2, pe_v, w_cv1, w_cv2, g_ffn, w_router, b_router, w_e1, b_e1, w_e2, b_e2, g_final):
    depth = w_ada.shape[0]
    assert depth == 1
    l = 0
    bp, t, d = x_prompt.shape
    bs, ts, _ = x_sample.shape

    w_all = _prep_w_in(w_in[l])
    w_out_b = w_out[l].astype(BF16)
    w1_b = w_e1[l].astype(BF16)
    w2_b = w_e2[l].astype(BF16)
    out_lw = (g_sb_out[l], g_nsa_out[l], w_out_b, g_ffn[l], w_router[l], b_router[l], w1_b, b_e1[l], w2_b, b_e2[l])

    n_c = bp + bs
    n_c_pad = -(-n_c // 8) * 8
    c_all = jnp.pad(jnp.concatenate([c_prompt, c_sample]), ((0, n_c_pad - n_c), (0, 0)))
    mod_all = _adaln(c_all, w_ada[l].astype(BF16), b_ada[l]).reshape(n_c_pad, 6, d)
    mp = [mod_all[:bp, i][:, None, :] for i in range(6)]
    ms = [jnp.repeat(mod_all[bp:n_c, i], ts, axis=0)[None] for i in range(6)]

    pos = jnp.arange(t)
    sq_b, sbkv, sbkv_b, nq_b, nsakv, winkv, dup_b, gates = _inproj(x_prompt, mp[0], mp[1], g_attn[l], w_all, pos)
    o_sb = _sb_prompt(sq_b, sbkv_b)
    rows4 = nsakv.reshape(bp, t // CMP_STRIDE, CMP_STRIDE, 4 * KV_WIDTH)
    kc = _compress(rows4, 0, pe_k[l], w_ck1[l], w_ck2[l], True)
    vc = _compress(rows4, 1, pe_v[l], w_cv1[l], w_cv2[l], False)
    o_nsa = _nsa_prompt(nq_b, gates, kc, vc, dup_b)
    y_prompt = _mix_out(x_prompt, o_sb, o_nsa, (mp[2], mp[3], mp[4], mp[5]), out_lw, g_final)
    keep = min(WINDOW, t)
    sb_kv_prompt = sbkv.reshape(1, bp, t, 2, N_HEADS_SB, HEAD_DIM)
    nsa_kv_prompt = nsakv.reshape(1, bp, t, 4, NSA_KV_HEADS, HEAD_DIM)
    win_kv_prompt = winkv[:, t - keep:].reshape(1, bp, keep, 2, NSA_KV_HEADS, HEAD_DIM)

    past_len = page_table.shape[1] * cache_sb_kv.shape[2]
    q_pos = past_len + jnp.arange(ts)
    pos_rows = jnp.tile(q_pos, bs)
    xs_flat = x_sample.reshape(1, bs * ts, d)
    s_sq_b, s_sbkv, _, s_nq_b, s_nsakv, s_winkv, _, s_gates = _inproj(xs_flat, ms[0], ms[1], g_attn[l], w_all, pos_rows)
    inv_scale = HEAD_DIM ** 0.5
    sq = (s_sq_b.astype(F32) * inv_scale).reshape(bs, ts, N_HEADS_SB, HEAD_DIM)
    nq = (s_nq_b.astype(F32) * inv_scale).reshape(bs, ts, N_HEADS_NSA, HEAD_DIM)
    sbkv_s = s_sbkv.reshape(bs, ts, 2, N_HEADS_SB, HEAD_DIM)
    nsakv_s = s_nsakv.reshape(bs, ts, 4, NSA_KV_HEADS, HEAD_DIM)
    winkv_s = s_winkv.reshape(bs, ts, 2, NSA_KV_HEADS, HEAD_DIM)
    gates_s = s_gates.reshape(bs, ts, LANES)[:, :, :N_GATES].reshape(bs, ts, N_HEADS_NSA, 3)
    past_pos = jnp.arange(past_len)
    o_sb_s = _sb_attend(sq, q_pos, [(_gather_pages(cache_sb_kv, l, page_table, 0),
                                     _gather_pages(cache_sb_kv, l, page_table, 1), past_pos),
                                    (sbkv_s[:, :, 0], sbkv_s[:, :, 1], q_pos)]).reshape(bs, ts, SB_WIDTH)
    ck_all = jnp.concatenate([_gather_pages(cache_nsa_kv, l, page_table, 0), nsakv_s[:, :, 0]], axis=1)
    cv_all = jnp.concatenate([_gather_pages(cache_nsa_kv, l, page_table, 1), nsakv_s[:, :, 1]], axis=1)
    kc_s, c_end = _compress_jax(ck_all, pe_k[l], w_ck1[l], w_ck2[l])
    vc_s, _ = _compress_jax(cv_all, pe_v[l], w_cv1[l], w_cv2[l])
    kc_s = _rope(kc_s, c_end)
    ks_all = jnp.concatenate([_gather_pages(cache_nsa_kv, l, page_table, 2), nsakv_s[:, :, 2]], axis=1)
    vs_all = jnp.concatenate([_gather_pages(cache_nsa_kv, l, page_table, 3), nsakv_s[:, :, 3]], axis=1)
    win_buf = state_win_kv[l]
    wl = win_buf.shape[1]
    kw = jnp.concatenate([win_buf[:, :, 0], winkv_s[:, :, 0]], axis=1)
    vw = jnp.concatenate([win_buf[:, :, 1], winkv_s[:, :, 1]], axis=1)
    w_pos = jnp.concatenate([past_len - wl + jnp.arange(wl), q_pos])
    o_nsa_s = _nsa_attend(nq, q_pos, kc_s, vc_s, c_end, ks_all, vs_all, kw, vw, w_pos, gates_s)
    y_sample = _mix_out(xs_flat, o_sb_s.reshape(1, bs * ts, SB_WIDTH), o_nsa_s.reshape(1, bs * ts, NSA_WIDTH),
                        (ms[2], ms[3], ms[4], ms[5]), out_lw, g_final).reshape(bs, ts, d)
    sb_kv_sample = sbkv_s[None]
    nsa_kv_sample = nsakv_s[None]
    win_kv_sample = jnp.stack([kw, vw], axis=2)[:, ts:][None]
    return (y_prompt, y_sample, sb_kv_prompt, nsa_kv_prompt, win_kv_prompt, sb_kv_sample, nsa_kv_sample, win_kv_sample)
```

```python
import functools

import jax
import jax.numpy as jnp
from jax import lax
from jax.experimental import pallas as pl
from jax.experimental.pallas import tpu as pltpu

HEAD_DIM = 64
N_HEADS_SB = 8
N_HEADS_NSA = 8
NSA_KV_HEADS = 2
NSA_GROUP = N_HEADS_NSA // NSA_KV_HEADS
SB_WIDTH = N_HEADS_SB * HEAD_DIM
NSA_WIDTH = N_HEADS_NSA * HEAD_DIM
KV_WIDTH = NSA_KV_HEADS * HEAD_DIM
N_GATES = 3 * N_HEADS_NSA
CMP_BLOCK = 32
CMP_STRIDE = 16
SLC_BLOCK = 64
SLC_TOP_N = 16
WINDOW = 512
N_EXPERTS = 32
TOP_K = 4
SWIGLU_LIMIT = 7.0
SWIGLU_ALPHA = 1.702
MOE_ROW_BLOCK = 256
ROPE_THETA = 10000.0
EPS = 1e-6
NEG = -1e30
FORCE_BONUS = 1e3
MASK_BIAS = -(2.0 ** 30)

LANES = 128
VMEM_LIMIT = 56 * 1024 * 1024

BF16 = jnp.bfloat16
F32 = jnp.float32


def _cparams(sem):
    return pltpu.CompilerParams(dimension_semantics=sem, vmem_limit_bytes=VMEM_LIMIT)


def _dot(a, b):
    return jnp.dot(a, b, preferred_element_type=F32)


def _dot_nt(a, b):
    return lax.dot_general(a, b, (((1,), (1,)), ((), ())), preferred_element_type=F32)


def _split2(x):
    hi = x.astype(BF16)
    lo = (x - hi.astype(F32)).astype(BF16)
    return hi, lo


def _rms(x, g):
    return (x * lax.rsqrt(jnp.mean(x * x, axis=-1, keepdims=True) + EPS)) * g


def _rope_slab(x, cos, sin_signed):
    lane = lax.broadcasted_iota(jnp.int32, x.shape, 1)
    first = (lane % HEAD_DIM) < (HEAD_DIM // 2)
    partner = jnp.where(first, pltpu.roll(x, LANES - HEAD_DIM // 2, 1), pltpu.roll(x, HEAD_DIM // 2, 1))
    return x * cos + partner * sin_signed


def _adaln_kernel(c_ref, w_ref, b_ref, o_ref):
    c = c_ref[...]
    s = (c * jax.nn.sigmoid(c)).astype(BF16)
    o_ref[...] = _dot(s, w_ref[...]) + b_ref[...]


def _adaln(c, w_ada_b, b_ada):
    r, d = c.shape
    n = w_ada_b.shape[1]
    tn = d
    return pl.pallas_call(
        _adaln_kernel,
        grid=(n // tn,),
        in_specs=[pl.BlockSpec((r, d), lambda j: (0, 0)),
                  pl.BlockSpec((d, tn), lambda j: (0, j)),
                  pl.BlockSpec((1, tn), lambda j: (0, j))],
        out_specs=pl.BlockSpec((r, tn), lambda j: (0, j)),
        out_shape=jax.ShapeDtypeStruct((r, n), F32),
        compiler_params=_cparams(("arbitrary",)),
        name="adaln",
    )(c, w_ada_b, b_ada.reshape(1, n))


_C_SQ = 0
_C_SBKV = SB_WIDTH
_C_NQ = 3 * SB_WIDTH
_C_NSAKV = _C_NQ + NSA_WIDTH
_C_WINKV = _C_NSAKV + 4 * KV_WIDTH
_C_DUP = _C_WINKV + 2 * KV_WIDTH
_C_GATE = _C_DUP + 8 * LANES
_C_END = _C_GATE + LANES


def _prep_w_in(w_in):
    main = w_in[:, :_C_DUP]
    dups = []
    for g in range(NSA_KV_HEADS):
        for slot in (2, 3, 4, 5):
            c0 = _C_NSAKV + slot * KV_WIDTH + g * HEAD_DIM
            col = w_in[:, c0:c0 + HEAD_DIM]
            dups += [col, col]
    gates = jnp.pad(w_in[:, _C_DUP:_C_DUP + N_GATES], ((0, 0), (0, LANES - N_GATES)))
    return jnp.concatenate([main] + dups + [gates], axis=1).astype(BF16)


def _inproj_kernel(x_ref, shift_ref, scale_ref, g_ref, w_ref, cos_ref, sin_ref,
                   sq_ref, sbkv_ref, sbkvb_ref, nq_ref, nsakv_ref, winkv_ref, dup_ref, gate_ref):
    x = x_ref[0]
    h = _rms(x, g_ref[...]) * (1.0 + scale_ref[0]) + shift_ref[0]
    hb = h.astype(BF16)
    cos = cos_ref[...]
    sin = sin_ref[...]
    qscale = HEAD_DIM ** -0.5

    def proj(c0, width):
        return _dot(hb, w_ref[:, c0:c0 + width])

    sq_ref[0] = (proj(_C_SQ, SB_WIDTH) * qscale).astype(BF16)
    kv = proj(_C_SBKV, 2 * SB_WIDTH)
    sbkv_ref[0] = kv
    sbkvb_ref[0] = kv.astype(BF16)
    for p in range(NSA_WIDTH // LANES):
        y = _rope_slab(proj(_C_NQ + p * LANES, LANES), cos, sin)
        nq_ref[0, :, p * LANES:(p + 1) * LANES] = (y * qscale).astype(BF16)
    for s in range(4):
        y = proj(_C_NSAKV + s * LANES, LANES)
        if s == 2:
            y = _rope_slab(y, cos, sin)
        nsakv_ref[0, :, s * LANES:(s + 1) * LANES] = y
    for s in range(2):
        y = proj(_C_WINKV + s * LANES, LANES)
        if s == 0:
            y = _rope_slab(y, cos, sin)
        winkv_ref[0, :, s * LANES:(s + 1) * LANES] = y
    for s in range(8):
        y = proj(_C_DUP + s * LANES, LANES)
        if s % 2 == 0:
            y = _rope_slab(y, cos, sin)
        dup_ref[0, :, s * LANES:(s + 1) * LANES] = y.astype(BF16)
    gate_ref[0] = jax.nn.sigmoid(proj(_C_GATE, LANES))


def _rope_tables(pos):
    half = HEAD_DIM // 2
    inv_freq = jnp.power(ROPE_THETA, -jnp.arange(half, dtype=F32) / half)
    ang = pos.astype(F32)[:, None] * inv_freq[None, :]
    cos, sin = jnp.cos(ang), jnp.sin(ang)
    cos_t = jnp.concatenate([cos, cos, cos, cos], axis=1)
    sin_t = jnp.concatenate([-sin, sin, -sin, sin], axis=1)
    return cos_t, sin_t


def _inproj(x, shift, scale, g_attn, w_all, pos):
    b, t, d = x.shape
    tm = min(256, t)
    tmod = shift.shape[1]
    cos_t, sin_t = _rope_tables(pos)
    if tmod == 1:
        mod_spec = pl.BlockSpec((1, 1, d), lambda i, j: (i, 0, 0))
    else:
        mod_spec = pl.BlockSpec((1, tm, d), lambda i, j: (i, j, 0))
    widths = (SB_WIDTH, 2 * SB_WIDTH, 2 * SB_WIDTH, NSA_WIDTH, 4 * KV_WIDTH, 2 * KV_WIDTH, 8 * LANES, LANES)
    dtypes = (BF16, F32, BF16, BF16, F32, F32, BF16, F32)
    return pl.pallas_call(
        _inproj_kernel,
        grid=(b, t // tm),
        in_specs=[pl.BlockSpec((1, tm, d), lambda i, j: (i, j, 0)),
                  mod_spec, mod_spec,
                  pl.BlockSpec((1, d), lambda i, j: (0, 0)),
                  pl.BlockSpec((d, _C_END), lambda i, j: (0, 0)),
                  pl.BlockSpec((tm, LANES), lambda i, j: (j, 0)),
                  pl.BlockSpec((tm, LANES), lambda i, j: (j, 0))],
        out_specs=[pl.BlockSpec((1, tm, w), lambda i, j: (i, j, 0)) for w in widths],
        out_shape=[jax.ShapeDtypeStruct((b, t, w), dt) for w, dt in zip(widths, dtypes)],
        compiler_params=_cparams(("arbitrary", "arbitrary")),
        name="inproj",
    )(x, shift, scale, g_attn.reshape(1, d), w_all, cos_t, sin_t)


def _sb_scores(q_h, k, tri, mask):
    z = _dot_nt(q_h, k)
    neg_log_keep = jnp.maximum(z, 0.0) + jnp.log(1.0 + jnp.exp(-jnp.abs(z)))
    if mask is not None:
        neg_log_keep = jnp.where(mask, neg_log_keep, 0.0)
    hi, lo = _split2(neg_log_keep)
    later = _dot(hi, tri) + _dot(lo, tri)
    return z - neg_log_keep - later, jnp.sum(neg_log_keep, axis=-1, keepdims=True)


def _sb_kernel(q_ref, k_ref, v_ref, o_ref, *, tq):
    qi = pl.program_id(2)
    q = q_ref[0]
    lane = lax.broadcasted_iota(jnp.int32, (1, LANES), 1)
    head_lanes = [lane < HEAD_DIM, lane >= HEAD_DIM]
    zero = jnp.zeros((), BF16)
    q_heads = [jnp.where(m, q, zero) for m in head_lanes]
    row = lax.broadcasted_iota(jnp.int32, (tq, tq), 0)
    col = lax.broadcasted_iota(jnp.int32, (tq, tq), 1)
    tri = (row > col).astype(BF16)
    diag_mask = col < row

    def tiles(js, carry, mask):
        rs = list(carry[:2])
        acc = carry[2]
        parts = []
        for j in js:
            start = pl.multiple_of(j * tq, tq)
            k = k_ref[0, pl.ds(start, tq), :]
            v = v_ref[0, pl.ds(start, tq), :]
            parts.append([(_sb_scores(q_heads[hh], k, tri, mask), jnp.where(head_lanes[hh], v, zero))
                          for hh in range(2)])
        for per_head in parts:
            for hh in range(2):
                (expo, total), v_h = per_head[hh]
                a = jnp.exp(expo - rs[hh])
                if mask is not None:
                    a = jnp.where(mask, a, 0.0)
                acc = acc + _dot(a.astype(BF16), v_h)
                rs[hh] = rs[hh] + total
        return rs[0], rs[1], acc

    init = (jnp.zeros((tq, 1), F32), jnp.zeros((tq, 1), F32), jnp.zeros((tq, LANES), F32))
    carry = tiles([qi], init, diag_mask)
    carry = lax.cond(qi % 2 == 1, lambda c: tiles([qi - 1], c, None), lambda c: c, carry)
    top = qi - qi % 2
    carry = lax.fori_loop(0, top // 2, lambda s, c: tiles([top - 1 - 2 * s, top - 2 - 2 * s], c, None), carry)
    o_ref[0] = carry[2]


def _sb_prompt(sq_b, sbkv_b):
    b, t, _ = sq_b.shape
    tq = min(256, t)
    n_pair = SB_WIDTH // LANES
    return pl.pallas_call(
        functools.partial(_sb_kernel, tq=tq),
        grid=(b, n_pair, t // tq),
        in_specs=[pl.BlockSpec((1, tq, LANES), lambda i, p, j: (i, j, p)),
                  pl.BlockSpec((1, t, LANES), lambda i, p, j: (i, 0, p)),
                  pl.BlockSpec((1, t, LANES), lambda i, p, j: (i, 0, n_pair + p))],
        out_specs=pl.BlockSpec((1, tq, LANES), lambda i, p, j: (i, j, p)),
        out_shape=jax.ShapeDtypeStruct((b, t, SB_WIDTH), F32),
        compiler_params=_cparams(("arbitrary", "arbitrary", "arbitrary")),
        name="sb_prompt",
    )(sq_b, sbkv_b, sbkv_b)


def _prep_cmp_weights(pe, w1, w2):
    hid = w1.shape[1]
    w1r = w1.reshape(CMP_BLOCK, HEAD_DIM, hid)
    z = jnp.zeros_like(w1r)
    w1_bd = jnp.concatenate([jnp.concatenate([w1r, z], axis=2), jnp.concatenate([z, w1r], axis=2)], axis=1)
    w2d = jnp.concatenate([w2, w2], axis=1)
    z2 = jnp.zeros_like(w2d)
    w2_bd = jnp.concatenate([jnp.concatenate([w2d, z2], axis=1), jnp.concatenate([z2, w2d], axis=1)], axis=0)
    pe_d = jnp.concatenate([pe, pe], axis=1)
    w1_pair = w1_bd.reshape(CMP_BLOCK // 2, 2 * LANES, 2 * hid)
    return pe_d, w1_pair.astype(BF16), w2_bd.astype(BF16)


def _compress_mlp(load_row, n, pe_ref, w1_ref, w2_ref, bot_ref, cos_sin, store):
    hid2 = w1_ref.shape[2]
    half = CMP_STRIDE // 2
    top = jnp.zeros((n, hid2), F32)
    bot = jnp.zeros((n, hid2), F32)
    for rp in range(half):
        xa = load_row(2 * rp)
        xb = load_row(2 * rp + 1)

        def lhs(off):
            r0 = off + 2 * rp
            return jnp.concatenate([xa + pe_ref[r0:r0 + 1, :], xb + pe_ref[r0 + 1:r0 + 2, :]], axis=1).astype(BF16)

        top = top + _dot(lhs(0), w1_ref[rp])
        bot = bot + _dot(lhs(CMP_STRIDE), w1_ref[half + rp])
    bot_ref[pl.ds(0, n), :] = bot
    bot_ref[pl.ds(n, 8), :] = jnp.zeros((8, hid2), F32)
    hidden = top + bot_ref[pl.ds(1, n), :]
    hidden = hidden * jax.nn.sigmoid(hidden)
    out = _dot(hidden.astype(BF16), w2_ref[...])
    for g in range(NSA_KV_HEADS):
        y = out[:, g * LANES:(g + 1) * LANES]
        if cos_sin is not None:
            y = _rope_slab(y, cos_sin[0][...], cos_sin[1][...])
        store(g, y.astype(BF16))


def _compress_kernel(x_ref, pe_ref, w1_ref, w2_ref, cos_ref, sin_ref, o_ref, bot_ref, *, rope):
    n = x_ref.shape[1]

    def store(g, y):
        o_ref[0, g] = y

    _compress_mlp(lambda r: x_ref[0, :, r, :], n, pe_ref, w1_ref, w2_ref, bot_ref,
                  (cos_ref, sin_ref) if rope else None, store)


def _compress(rows4, slot, pe, w1, w2, rope):
    b, n, _, _ = rows4.shape
    pe_d, w1_bd, w2_bd = _prep_cmp_weights(pe, w1, w2)
    c_end = jnp.arange(n) * CMP_STRIDE + CMP_BLOCK - 1
    cos_t, sin_t = _rope_tables(c_end)
    hid2 = w1_bd.shape[2]
    return pl.pallas_call(
        functools.partial(_compress_kernel, rope=rope),
        grid=(b,),
        in_specs=[pl.BlockSpec((1, n, CMP_STRIDE, LANES), lambda i: (i, 0, 0, slot)),
                  pl.BlockSpec((CMP_BLOCK, LANES), lambda i: (0, 0)),
                  pl.BlockSpec((CMP_BLOCK // 2, 2 * LANES, hid2), lambda i: (0, 0, 0)),
                  pl.BlockSpec((hid2, 2 * LANES), lambda i: (0, 0)),
                  pl.BlockSpec((n, LANES), lambda i: (0, 0)),
                  pl.BlockSpec((n, LANES), lambda i: (0, 0))],
        out_specs=pl.BlockSpec((1, NSA_KV_HEADS, n, LANES), lambda i: (i, 0, 0, 0)),
        out_shape=jax.ShapeDtypeStruct((b, NSA_KV_HEADS, n, LANES), BF16),
        scratch_shapes=[pltpu.VMEM((n + 8, hid2), F32)],
        compiler_params=_cparams(("arbitrary",)),
        name="compress",
    )(rows4, pe_d, w1_bd, w2_bd, cos_t, sin_t)


def _nsa_kernel(q_ref, gate_ref, kc_ref, vc_ref, ovl_ref, ks_ref, vs_ref, kw_ref, vw_ref, o_ref, *, tq, tk, n_top):
    g = pl.program_id(1)
    qi = pl.program_id(2)
    t_len = ks_ref.shape[2]
    n_c = kc_ref.shape[2]
    n_slc = ovl_ref.shape[0]
    s0 = qi * tq
    zero = jnp.zeros((), BF16)
    lane = lax.broadcasted_iota(jnp.int32, (1, LANES), 1)
    head_lanes = [lane < HEAD_DIM, lane >= HEAD_DIM]
    q = q_ref[0]
    q_rows = jnp.concatenate(
        [jnp.where(head_lanes[h % 2], q[:, (h // 2) * LANES:(h // 2 + 1) * LANES], zero) for h in range(NSA_GROUP)],
        axis=0)
    q_pos = s0 + lax.broadcasted_iota(jnp.int32, (tq, 1), 0)
    q_pos4 = jnp.concatenate([q_pos] * NSA_GROUP, axis=0)

    def head_merge(o_rows):
        slabs = []
        for p in range(NSA_GROUP // 2):
            a = o_rows[(2 * p) * tq:(2 * p + 1) * tq]
            b2 = o_rows[(2 * p + 1) * tq:(2 * p + 2) * tq]
            slabs.append(jnp.where(head_lanes[0], a, b2))
        return jnp.concatenate(slabs, axis=1)

    kc = kc_ref[0, 0]
    vc = vc_ref[0, 0]
    s = _dot_nt(q_rows, kc)
    c_idx = lax.broadcasted_iota(jnp.int32, (1, n_c), 1)
    c_end = c_idx * CMP_STRIDE + (CMP_BLOCK - 1)
    c_mask = (c_end <= q_pos4) & (c_idx < n_c - 1)
    s = jnp.where(c_mask, s, NEG)
    m = jnp.max(s, axis=-1, keepdims=True)
    e = jnp.where(c_mask, jnp.exp(s - m), 0.0)
    p_cmp = e / jnp.maximum(jnp.sum(e, axis=-1, keepdims=True), 1.0)
    o_cmp = head_merge(_dot(p_cmp.astype(BF16), vc))

    p_sum = p_cmp[0:tq]
    for h in range(1, NSA_GROUP):
        p_sum = p_sum + p_cmp[h * tq:(h + 1) * tq]
    ovl = ovl_ref[...]
    p_hi, p_lo = _split2(p_sum)
    p_lo2 = (p_sum - p_hi.astype(F32) - p_lo.astype(F32)).astype(BF16)
    imp = _dot_nt(ovl, p_hi) + _dot_nt(ovl, p_lo) + _dot_nt(ovl, p_lo2)
    blk = lax.broadcasted_iota(jnp.int32, (n_slc, 1), 0)
    qp_l = s0 + lax.broadcasted_iota(jnp.int32, (1, tq), 1)
    cur = qp_l // SLC_BLOCK
    forced = (blk == 0) | (blk == cur) | (blk == cur - 1)
    imp = jnp.where(blk * SLC_BLOCK <= qp_l, imp + jnp.where(forced, FORCE_BONUS, 0.0), NEG)
    blk_full = lax.broadcasted_iota(jnp.int32, (n_slc, tq), 0)

    def pick(_, carry):
        imp_c, unsel = carry
        best = jnp.max(imp_c, axis=0, keepdims=True)
        first = jnp.min(jnp.where(imp_c == best, blk_full, n_slc), axis=0, keepdims=True)
        hit = blk_full == first
        return jnp.where(hit, -jnp.inf, imp_c), jnp.where(hit, 0.0, unsel)

    _, unsel_t = lax.fori_loop(0, n_top, pick, (imp, jnp.ones((n_slc, tq), F32)))
    unsel = unsel_t.T.astype(BF16)
    q_aug = jnp.concatenate([q_rows, jnp.concatenate([unsel] * NSA_GROUP, axis=0)], axis=1)

    blocks_per_tile = tk // SLC_BLOCK
    k_row_blk = lax.broadcasted_iota(jnp.int32, (tk, n_slc), 0) // SLC_BLOCK
    k_lane = lax.broadcasted_iota(jnp.int32, (tk, n_slc), 1)
    k_off = lax.broadcasted_iota(jnp.int32, (1, tk), 1)

    def slc_tile(j, carry):
        m_run, l_run, acc = carry
        start = pl.multiple_of(j * tk, tk)
        k = ks_ref[0, 0, pl.ds(start, tk), :]
        v = vs_ref[0, 0, pl.ds(start, tk), :]
        onehot = jnp.where(k_lane == k_row_blk + j * blocks_per_tile, MASK_BIAS, 0.0).astype(BF16)
        k_aug = jnp.concatenate([k, onehot], axis=1)
        sc = _dot_nt(q_aug, k_aug)
        sc = jnp.where(start + k_off <= q_pos4, sc, MASK_BIAS)
        m_new = jnp.maximum(m_run, jnp.max(sc, axis=-1, keepdims=True))
        alpha = jnp.exp(m_run - m_new)
        pr = jnp.exp(sc - m_new)
        l_new = alpha * l_run + jnp.sum(pr, axis=-1, keepdims=True)
        acc = alpha * acc + _dot(pr.astype(BF16), v)
        return m_new, l_new, acc

    n_tiles = (s0 + tq + tk - 1) // tk
    init = (jnp.full((NSA_GROUP * tq, 1), NEG, F32), jnp.zeros((NSA_GROUP * tq, 1), F32),
            jnp.zeros((NSA_GROUP * tq, LANES), F32))
    _, l_fin, acc = lax.fori_loop(0, n_tiles, slc_tile, init)
    o_slc = head_merge(acc / jnp.maximum(l_fin, 1.0))

    band = min(WINDOW + tq, t_len)
    w_start = jnp.clip(s0 - WINDOW, 0, t_len - band)
    w_start = pl.multiple_of(w_start, tq)
    kw = kw_ref[0, 0, pl.ds(w_start, band), :]
    vw = vw_ref[0, 0, pl.ds(w_start, band), :]
    sw = _dot_nt(q_rows, kw)
    w_pos = w_start + lax.broadcasted_iota(jnp.int32, (1, band), 1)
    dist = q_pos4 - w_pos
    w_mask = (dist >= 0) & (dist < WINDOW)
    sw = jnp.where(w_mask, sw, NEG)
    mw = jnp.max(sw, axis=-1, keepdims=True)
    ew = jnp.where(w_mask, jnp.exp(sw - mw), 0.0)
    pw = ew / jnp.maximum(jnp.sum(ew, axis=-1, keepdims=True), 1.0)
    o_win = head_merge(_dot(pw.astype(BF16), vw))

    gates = gate_ref[0]
    g_hi, g_lo = _split2(gates)
    col = lax.broadcasted_iota(jnp.int32, (LANES, NSA_GROUP * HEAD_DIM), 0)
    head_of_lane = lax.broadcasted_iota(jnp.int32, (LANES, NSA_GROUP * HEAD_DIM), 1) // HEAD_DIM + g * NSA_GROUP
    out = jnp.zeros((tq, NSA_GROUP * HEAD_DIM), F32)
    for branch, o_b in enumerate((o_cmp, o_slc, o_win)):
        expand = (col == head_of_lane * 3 + branch).astype(BF16)
        out = out + (_dot(g_hi, expand) + _dot(g_lo, expand)) * o_b
    o_ref[0] = out


def _cmp_overlap_t(n_c, n_slc):
    cs = jnp.arange(n_c) * CMP_STRIDE
    ss = jnp.arange(n_slc) * SLC_BLOCK
    ok = (cs[None, :] < ss[:, None] + SLC_BLOCK) & (cs[None, :] + CMP_BLOCK > ss[:, None]) & (jnp.arange(n_c)[None, :] < n_c - 1)
    return ok.astype(BF16)


def _nsa_prompt(nq_b, gates, kc, vc, dup_b):
    b, t, _ = nq_b.shape
    n_c = kc.shape[2]
    n_slc = t // SLC_BLOCK
    tq = 128
    tk = min(512, t)
    n_top = min(SLC_TOP_N, n_slc)
    gw = NSA_GROUP * HEAD_DIM
    kv_spec = lambda slot: pl.BlockSpec((1, 1, t, LANES), lambda i, g, j: (i, 0, 0, g * 4 + slot))
    dup4 = dup_b.reshape(b, 1, t, 8 * LANES)
    return pl.pallas_call(
        functools.partial(_nsa_kernel, tq=tq, tk=tk, n_top=n_top),
        grid=(b, NSA_KV_HEADS, t // tq),
        in_specs=[pl.BlockSpec((1, tq, gw), lambda i, g, j: (i, j, g)),
                  pl.BlockSpec((1, tq, LANES), lambda i, g, j: (i, j, 0)),
                  pl.BlockSpec((1, 1, n_c, LANES), lambda i, g, j: (i, g, 0, 0)),
                  pl.BlockSpec((1, 1, n_c, LANES), lambda i, g, j: (i, g, 0, 0)),
                  pl.BlockSpec((n_slc, n_c), lambda i, g, j: (0, 0)),
                  kv_spec(0), kv_spec(1), kv_spec(2), kv_spec(3)],
        out_specs=pl.BlockSpec((1, tq, gw), lambda i, g, j: (i, j, g)),
        out_shape=jax.ShapeDtypeStruct((b, t, NSA_WIDTH), F32),
        compiler_params=_cparams(("arbitrary", "arbitrary", "arbitrary")),
        name="nsa_prompt",
    )(nq_b, gates, kc, vc, _cmp_overlap_t(n_c, n_slc), dup4, dup4, dup4, dup4)


def _outproj_kernel(x_ref, osb_ref, onsa_ref, gate_ref, shift_ref, scale_ref, gsb_ref, gnsa_ref, gffn_ref,
                    wo_ref, wr_ref, br_ref, x1_ref, h_ref, logit_ref):
    o_sb = _rms(osb_ref[0], gsb_ref[...]).astype(BF16)
    o_nsa = _rms(onsa_ref[0], gnsa_ref[...]).astype(BF16)
    mixed = _dot(o_sb, wo_ref[0:SB_WIDTH, :]) + _dot(o_nsa, wo_ref[SB_WIDTH:, :])
    x1 = x_ref[0] + gate_ref[0] * mixed
    x1_ref[0] = x1
    h = _rms(x1, gffn_ref[...]) * (1.0 + scale_ref[0]) + shift_ref[0]
    h_ref[0] = h.astype(BF16)
    h_hi, h_lo = _split2(h)
    logit_ref[0] = _dot(h_hi, wr_ref[0]) + _dot(h_lo, wr_ref[0]) + _dot(h_hi, wr_ref[1]) + br_ref[...]


def _outproj(x, o_sb, o_nsa, gate, shift, scale, g_sb, g_nsa, g_ffn, w_out_b, w_router, b_router):
    b, t, d = x.shape
    tm = min(256, t)
    tmod = gate.shape[1]
    if tmod == 1:
        mod_spec = pl.BlockSpec((1, 1, d), lambda i, j: (i, 0, 0))
    else:
        mod_spec = pl.BlockSpec((1, tm, d), lambda i, j: (i, j, 0))
    wr = jnp.pad(w_router, ((0, 0), (0, LANES - N_EXPERTS)))
    wr_hi = wr.astype(BF16)
    wr_lo = (wr - wr_hi.astype(F32)).astype(BF16)
    wr2 = jnp.stack([wr_hi, wr_lo])
    br = jnp.pad(b_router, (0, LANES - N_EXPERTS)).reshape(1, LANES)
    row = lambda w: pl.BlockSpec((1, tm, w), lambda i, j: (i, j, 0))
    vec = lambda w: pl.BlockSpec((1, w), lambda i, j: (0, 0))
    return pl.pallas_call(
        _outproj_kernel,
        grid=(b, t // tm),
        in_specs=[row(d), row(SB_WIDTH), row(NSA_WIDTH), mod_spec, mod_spec, mod_spec,
                  vec(SB_WIDTH), vec(NSA_WIDTH), vec(d),
                  pl.BlockSpec((SB_WIDTH + NSA_WIDTH, d), lambda i, j: (0, 0)),
                  pl.BlockSpec((2, d, LANES), lambda i, j: (0, 0, 0)),
                  vec(LANES)],
        out_specs=[row(d), row(d), row(LANES)],
        out_shape=[jax.ShapeDtypeStruct((b, t, d), F32), jax.ShapeDtypeStruct((b, t, d), BF16),
                   jax.ShapeDtypeStruct((b, t, LANES), F32)],
        compiler_params=_cparams(("arbitrary", "arbitrary")),
        name="outproj",
    )(x, o_sb, o_nsa, gate, shift, scale, g_sb.reshape(1, -1), g_nsa.reshape(1, -1), g_ffn.reshape(1, -1),
      w_out_b, wr2, br)


def _moe_kernel(be_ref, xs_ref, gate_ref, w1_ref, b1_ref, w2_ref, b2_ref, o_ref):
    del be_ref
    d_ff = w2_ref.shape[1]
    gu = _dot(xs_ref[...], w1_ref[0]) + b1_ref[0]
    gl = jnp.minimum(gu[:, :d_ff], SWIGLU_LIMIT)
    u = jnp.clip(gu[:, d_ff:], -SWIGLU_LIMIT, SWIGLU_LIMIT)
    act = (u + 1.0) * gl * jax.nn.sigmoid(SWIGLU_ALPHA * gl)
    y = _dot(act.astype(BF16), w2_ref[0]) + b2_ref[0]
    o_ref[...] = y * gate_ref[...]


def _moe_ffn(xs, row_gate, block_expert, w1_b, b1, w2_b, b2, rb):
    rows, d = xs.shape
    n_blocks = rows // rb
    e, _, f2 = w1_b.shape
    grid_spec = pltpu.PrefetchScalarGridSpec(
        num_scalar_prefetch=1,
        grid=(n_blocks,),
        in_specs=[pl.BlockSpec((rb, d), lambda i, be: (i, 0)),
                  pl.BlockSpec((rb, 1), lambda i, be: (i, 0)),
                  pl.BlockSpec((1, d, f2), lambda i, be: (be[i], 0, 0)),
                  pl.BlockSpec((1, 1, f2), lambda i, be: (be[i], 0, 0)),
                  pl.BlockSpec((1, f2 // 2, d), lambda i, be: (be[i], 0, 0)),
                  pl.BlockSpec((1, 1, d), lambda i, be: (be[i], 0, 0))],
        out_specs=pl.BlockSpec((rb, d), lambda i, be: (i, 0)),
    )
    return pl.pallas_call(
        _moe_kernel,
        grid_spec=grid_spec,
        out_shape=jax.ShapeDtypeStruct((rows, d), F32),
        compiler_params=_cparams(("arbitrary",)),
        name="moe_ffn",
    )(block_expert, xs, row_gate.reshape(rows, 1), w1_b, b1.reshape(e, 1, f2), w2_b, b2.reshape(e, 1, d))


def _moe(h_b, logits, w1_b, b1, w2_b, b2):
    n, d = h_b.shape
    a = n * TOP_K
    top_val, top_idx = lax.top_k(logits, TOP_K)
    gate = jax.nn.softmax(top_val, axis=-1)
    e_flat = top_idx.reshape(a)
    tok_flat = jnp.repeat(jnp.arange(n, dtype=jnp.int32), TOP_K)
    g_flat = gate.reshape(a)
    rb = MOE_ROW_BLOCK if a >= MOE_ROW_BLOCK * N_EXPERTS else max(8, a // N_EXPERTS)
    order = jnp.argsort(e_flat)
    e_sorted = e_flat[order]
    counts = jnp.bincount(e_flat, length=N_EXPERTS)
    padded = (counts + rb - 1) // rb * rb
    start = jnp.cumsum(counts) - counts
    pad_end = jnp.cumsum(padded)
    pad_start = pad_end - padded
    dest = pad_start[e_sorted] + jnp.arange(a) - start[e_sorted]
    n_blocks = -(-a // rb) + N_EXPERTS
    rows = n_blocks * rb
    row_tok = jnp.full((rows,), n, jnp.int32).at[dest].set(tok_flat[order])
    row_gate = jnp.zeros((rows,), F32).at[dest].set(g_flat[order])
    block_expert = jnp.minimum(jnp.searchsorted(pad_end, jnp.arange(n_blocks) * rb, side='right'),
                               N_EXPERTS - 1).astype(jnp.int32)
    xs = jnp.concatenate([h_b, jnp.zeros((1, d), h_b.dtype)])[row_tok]
    ys = _moe_ffn(xs, row_gate, block_expert, w1_b, b1, w2_b, b2, rb)
    return jnp.zeros((n + 1, d), F32).at[row_tok].add(ys)[:n]


def _final_kernel(x1_ref, y_ref, gate_ref, g_ref, o_ref):
    o_ref[0] = _rms(x1_ref[0] + gate_ref[0] * y_ref[0], g_ref[...])


def _final(x1, y, gate, g_final):
    b, t, d = x1.shape
    tm = min(512, t)
    tmod = gate.shape[1]
    if tmod == 1:
        mod_spec = pl.BlockSpec((1, 1, d), lambda i, j: (i, 0, 0))
    else:
        mod_spec = pl.BlockSpec((1, tm, d), lambda i, j: (i, j, 0))
    row = pl.BlockSpec((1, tm, d), lambda i, j: (i, j, 0))
    return pl.pallas_call(
        _final_kernel,
        grid=(b, t // tm),
        in_specs=[row, row, mod_spec, pl.BlockSpec((1, d), lambda i, j: (0, 0))],
        out_specs=row,
        out_shape=jax.ShapeDtypeStruct((b, t, d), F32),
        compiler_params=_cparams(("arbitrary", "arbitrary")),
        name="final",
    )(x1, y, gate, g_final.reshape(1, d))


PAGES_PER_STEP = 8


def _page_specs(pg, n_steps, page_rows, width, reverse):
    def spec(i):
        def index(b, s, pt):
            grp = (n_steps - 1 - s) if reverse else s
            return (pt[b, grp * pg + i], 0, 0)
        return pl.BlockSpec((1, page_rows, width), index)
    return [spec(i) for i in range(pg)]


def _sb_sample_kernel(pt_ref, q_ref, kn_ref, vn_ref, *rest, pg, ts):
    del pt_ref
    pages = rest[:pg]
    o_ref, r_ref, acc_ref = rest[pg:]
    s = pl.program_id(1)
    n_rows = q_ref.shape[1]
    seg = 2 * LANES
    q = q_ref[0]
    tri = (lax.broadcasted_iota(jnp.int32, (seg, seg), 0) > lax.broadcasted_iota(jnp.int32, (seg, seg), 1)).astype(BF16)
    q_tok = lax.broadcasted_iota(jnp.int32, (n_rows, 1), 0) % ts

    def tile(k, v, r, acc, mask):
        nseg = k.shape[0] // seg
        z = _dot_nt(q, k)
        neg_log_keep = jnp.maximum(z, 0.0) + jnp.log(1.0 + jnp.exp(-jnp.abs(z)))
        if mask is not None:
            neg_log_keep = jnp.where(mask, neg_log_keep, 0.0)
        segs = [neg_log_keep[:, g * seg:(g + 1) * seg] for g in range(nseg)]
        stacked = segs[0] if nseg == 1 else jnp.concatenate(segs, axis=0)
        hi, lo = _split2(stacked)
        later = _dot(hi, tri) + _dot(lo, tri)
        offs = [None] * nseg
        off = r
        for g in reversed(range(nseg)):
            offs[g] = off
            off = off + jnp.sum(segs[g], axis=-1, keepdims=True)
        parts = [later[g * n_rows:(g + 1) * n_rows] + offs[g] for g in range(nseg)]
        cs = parts[0] if nseg == 1 else jnp.concatenate(parts, axis=1)
        a = jnp.exp(z - neg_log_keep - cs)
        if mask is not None:
            a = jnp.where(mask, a, 0.0)
        return off, acc + _dot(a.astype(BF16), v)

    @pl.when(s == 0)
    def _():
        key = lax.broadcasted_iota(jnp.int32, (1, seg), 1)
        r, acc = tile(kn_ref[0], vn_ref[0], jnp.zeros((n_rows, 1), F32), jnp.zeros((n_rows, SB_WIDTH), F32),
                      key < q_tok)
        r_ref[...] = jnp.broadcast_to(r, r_ref.shape)
        acc_ref[...] = acc

    k = jnp.concatenate([p[0, :, :SB_WIDTH] for p in pages], axis=0).astype(BF16)
    v = jnp.concatenate([p[0, :, SB_WIDTH:] for p in pages], axis=0).astype(BF16)
    r, acc = tile(k, v, r_ref[:, 0:1], acc_ref[...], None)
    r_ref[...] = jnp.broadcast_to(r, r_ref.shape)
    acc_ref[...] = acc

    @pl.when(s == pl.num_programs(1) - 1)
    def _():
        lane_head = lax.broadcasted_iota(jnp.int32, (n_rows, SB_WIDTH), 1) // HEAD_DIM
        row_head = lax.broadcasted_iota(jnp.int32, (n_rows, SB_WIDTH), 0) // ts
        own = jnp.where(lane_head == row_head, acc, 0.0)
        out = own[0:ts]
        for h in range(1, N_HEADS_SB):
            out = out + own[h * ts:(h + 1) * ts]
        o_ref[0] = out


def _sb_sample(sq_b, sbkv_new_b, pool, page_table):
    bs, ts, _ = sq_b.shape
    n_pages = page_table.shape[1]
    page_rows = pool.shape[1]
    pg = min(PAGES_PER_STEP, n_pages)
    n_steps = n_pages // pg
    n_rows = N_HEADS_SB * ts
    seg = 2 * LANES
    eye = jnp.eye(N_HEADS_SB, dtype=sq_b.dtype)
    q_rows = jnp.einsum('bqhd,hg->bhqgd', sq_b.reshape(bs, ts, N_HEADS_SB, HEAD_DIM), eye).reshape(bs, n_rows, SB_WIDTH)
    new = jnp.pad(sbkv_new_b, ((0, 0), (0, seg - ts), (0, 0)))
    k_new, v_new = new[:, :, :SB_WIDTH], new[:, :, SB_WIDTH:]
    per_b = lambda rows, w: pl.BlockSpec((1, rows, w), lambda b, s, pt: (b, 0, 0))
    grid_spec = pltpu.PrefetchScalarGridSpec(
        num_scalar_prefetch=1,
        grid=(bs, n_steps),
        in_specs=[per_b(n_rows, SB_WIDTH), per_b(seg, SB_WIDTH), per_b(seg, SB_WIDTH)]
        + _page_specs(pg, n_steps, page_rows, 2 * SB_WIDTH, True),
        out_specs=per_b(ts, SB_WIDTH),
        scratch_shapes=[pltpu.VMEM((n_rows, LANES), F32), pltpu.VMEM((n_rows, SB_WIDTH), F32)],
    )
    return pl.pallas_call(
        functools.partial(_sb_sample_kernel, pg=pg, ts=ts),
        grid_spec=grid_spec,
        out_shape=jax.ShapeDtypeStruct((bs, ts, SB_WIDTH), F32),
        compiler_params=_cparams(("arbitrary", "arbitrary")),
        name="sb_sample",
    )(page_table, q_rows, k_new, v_new, *([pool] * pg))


def _nsa_stage_kernel(pt_ref, new_ref, pek_ref, w1k_ref, w2k_ref, pev_ref, w1v_ref, w2v_ref, cos_ref, sin_ref, *rest,
                      pg, n_steps):
    del pt_ref
    pages = rest[:pg]
    kc_ref, vc_ref, lkv_ref, rows_ref, bot_ref = rest[pg:]
    s = pl.program_id(1)
    n_cp = rows_ref.shape[0]
    page_rows = pages[0].shape[1]
    cpp = page_rows // CMP_STRIDE
    n_past = n_steps * pg * cpp
    cw = 2 * KV_WIDTH

    @pl.when(s == 0)
    def _():
        rows_ref[pl.ds(n_past, n_cp - n_past)] = jnp.zeros((n_cp - n_past, CMP_STRIDE, cw), F32)
        rows_ref[n_past] = new_ref[0]

    for i, p in enumerate(pages):
        base = pl.multiple_of((s * pg + i) * cpp, cpp)
        rows_ref[pl.ds(base, cpp)] = p[0, :, 0:cw].reshape(cpp, CMP_STRIDE, cw)
        lkv_ref[0, i * page_rows:(i + 1) * page_rows, :] = p[0, :, cw:2 * cw].astype(BF16)

    @pl.when(s == n_steps - 1)
    def _():
        def store_k(g, y):
            kc_ref[0, g] = y

        def store_v(g, y):
            vc_ref[0, g] = y

        _compress_mlp(lambda r: rows_ref[:, r, 0:KV_WIDTH], n_cp, pek_ref, w1k_ref, w2k_ref, bot_ref,
                      (cos_ref, sin_ref), store_k)
        _compress_mlp(lambda r: rows_ref[:, r, KV_WIDTH:cw], n_cp, pev_ref, w1v_ref, w2v_ref, bot_ref, None, store_v)


def _nsa_stage(pool, page_table, new_cmp, cmp_k, cmp_v):
    bs, n_pages = page_table.shape
    page_rows = pool.shape[1]
    pg = min(PAGES_PER_STEP, n_pages)
    n_steps = n_pages // pg
    cpp = page_rows // CMP_STRIDE
    n_past = n_pages * cpp
    n_cp = n_past + 8
    pe_k, w1k, w2k = _prep_cmp_weights(*cmp_k)
    pe_v, w1v, w2v = _prep_cmp_weights(*cmp_v)
    hid2 = w1k.shape[2]
    cos_t, sin_t = _rope_tables(jnp.arange(n_cp) * CMP_STRIDE + CMP_BLOCK - 1)
    cw = 2 * KV_WIDTH
    const = lambda shape: pl.BlockSpec(shape, lambda b, s, pt: (0,) * len(shape))
    w_specs = [const((CMP_BLOCK, LANES)), const((CMP_BLOCK // 2, 2 * LANES, hid2)), const((hid2, 2 * LANES))]
    out_c = pl.BlockSpec((1, NSA_KV_HEADS, n_cp, LANES), lambda b, s, pt: (b, 0, 0, 0))
    grid_spec = pltpu.PrefetchScalarGridSpec(
        num_scalar_prefetch=1,
        grid=(bs, n_steps),
        in_specs=[pl.BlockSpec((1, CMP_STRIDE, cw), lambda b, s, pt: (b, 0, 0))] + w_specs + w_specs
        + [const((n_cp, LANES)), const((n_cp, LANES))] + _page_specs(pg, n_steps, page_rows, 4 * KV_WIDTH, False),
        out_specs=[out_c, out_c, pl.BlockSpec((1, pg * page_rows, cw), lambda b, s, pt: (b, s, 0))],
        scratch_shapes=[pltpu.VMEM((n_cp, CMP_STRIDE, cw), F32), pltpu.VMEM((n_cp + 8, hid2), F32)],
    )
    c_shape = jax.ShapeDtypeStruct((bs, NSA_KV_HEADS, n_cp, LANES), BF16)
    return pl.pallas_call(
        functools.partial(_nsa_stage_kernel, pg=pg, n_steps=n_steps),
        grid_spec=grid_spec,
        out_shape=[c_shape, c_shape, jax.ShapeDtypeStruct((bs, n_pages * page_rows, cw), BF16)],
        compiler_params=_cparams(("arbitrary", "arbitrary")),
        name="nsa_stage",
    )(page_table, new_cmp, pe_k, w1k, w2k, pe_v, w1v, w2v, cos_t, sin_t, *([pool] * pg))


def _nsa_sample_kernel(q_ref, gate_ref, kc_ref, vc_ref, ovl_ref, lkv_ref, lkvn_ref, win_ref, winn_ref, o_ref, *,
                       ts, past_len, n_cmp, n_slc, n_top, tk):
    q = q_ref[0]
    n_rows = q.shape[0]
    gr = n_rows // NSA_KV_HEADS
    n_cp = kc_ref.shape[2]
    n_sp = ovl_ref.shape[1]
    q_tok = lax.broadcasted_iota(jnp.int32, (n_rows, 1), 0) % ts
    q_pos = past_len + q_tok

    c_idx = lax.broadcasted_iota(jnp.int32, (1, n_cp), 1)
    c_mask = (c_idx * CMP_STRIDE + (CMP_BLOCK - 1) <= q_pos) & (c_idx < n_cmp)
    s = jnp.concatenate([_dot_nt(q[g * gr:(g + 1) * gr], kc_ref[0, g]) for g in range(NSA_KV_HEADS)], axis=0)
    s = jnp.where(c_mask, s, NEG)
    m = jnp.max(s, axis=-1, keepdims=True)
    e = jnp.where(c_mask, jnp.exp(s - m), 0.0)
    p_cmp = e / jnp.maximum(jnp.sum(e, axis=-1, keepdims=True), 1.0)
    p_b = p_cmp.astype(BF16)
    o_cmp = jnp.concatenate([_dot(p_b[g * gr:(g + 1) * gr], vc_ref[0, g]) for g in range(NSA_KV_HEADS)], axis=0)

    sums = []
    for g in range(NSA_KV_HEADS):
        acc = p_cmp[g * gr:g * gr + ts]
        for h in range(1, NSA_GROUP):
            acc = acc + p_cmp[g * gr + h * ts:g * gr + (h + 1) * ts]
        sums.append(acc)
    p_sum = jnp.concatenate(sums, axis=0)
    n_imp = p_sum.shape[0]
    p_hi, p_lo = _split2(p_sum)
    p_lo2 = (p_sum - p_hi.astype(F32) - p_lo.astype(F32)).astype(BF16)
    ovl = ovl_ref[...]
    imp = _dot(p_hi, ovl) + _dot(p_lo, ovl) + _dot(p_lo2, ovl)
    blk = lax.broadcasted_iota(jnp.int32, (n_imp, n_sp), 1)
    qp = past_len + lax.broadcasted_iota(jnp.int32, (n_imp, 1), 0) % ts
    cur = qp // SLC_BLOCK
    forced = (blk == 0) | (blk == cur) | (blk == cur - 1)
    imp = jnp.where(blk * SLC_BLOCK <= qp, imp + jnp.where(forced, FORCE_BONUS, 0.0), NEG)
    imp = jnp.where(blk < n_slc, imp, -jnp.inf)

    def pick(_, carry):
        imp_c, unsel = carry
        best = jnp.max(imp_c, axis=-1, keepdims=True)
        first = jnp.min(jnp.where(imp_c == best, blk, n_sp), axis=-1, keepdims=True)
        hit = blk == first
        return jnp.where(hit, -jnp.inf, imp_c), jnp.where(hit, 0.0, unsel)

    _, unsel = lax.fori_loop(0, n_top, pick, (imp, jnp.ones((n_imp, n_sp), F32)))
    unsel = unsel.astype(BF16)
    unsel_rows = jnp.concatenate([unsel[g * ts:(g + 1) * ts] for g in range(NSA_KV_HEADS) for _ in range(NSA_GROUP)],
                                 axis=0)
    q_aug = jnp.concatenate([q, unsel_rows], axis=1)

    def update(carry, sc, v):
        m_run, l_run, acc = carry
        m_new = jnp.maximum(m_run, jnp.max(sc, axis=-1, keepdims=True))
        alpha = jnp.exp(m_run - m_new)
        pr = jnp.exp(sc - m_new)
        return m_new, alpha * l_run + jnp.sum(pr, axis=-1, keepdims=True), alpha * acc + _dot(pr.astype(BF16), v)

    def bias_cols(first_pos, rows):
        key_blk = (first_pos + lax.broadcasted_iota(jnp.int32, (rows, n_sp), 0)) // SLC_BLOCK
        return jnp.where(lax.broadcasted_iota(jnp.int32, (rows, n_sp), 1) == key_blk, MASK_BIAS, 0.0).astype(BF16)

    def slc_tile(j, carry):
        start = pl.multiple_of(j * tk, tk)
        k = lkv_ref[0, pl.ds(start, tk), 0:KV_WIDTH]
        v = lkv_ref[0, pl.ds(start, tk), KV_WIDTH:2 * KV_WIDTH]
        sc = _dot_nt(q_aug, jnp.concatenate([k, bias_cols(start, tk)], axis=1))
        return update(carry, sc, v)

    init = (jnp.full((n_rows, 1), NEG, F32), jnp.zeros((n_rows, 1), F32), jnp.zeros((n_rows, KV_WIDTH), F32))
    carry = lax.fori_loop(0, past_len // tk, slc_tile, init)
    n_new = lkvn_ref.shape[1]
    new_idx = lax.broadcasted_iota(jnp.int32, (1, n_new), 1)
    kn = lkvn_ref[0, :, 0:KV_WIDTH]
    vn = lkvn_ref[0, :, KV_WIDTH:2 * KV_WIDTH]
    sc = _dot_nt(q_aug, jnp.concatenate([kn, bias_cols(past_len, n_new)], axis=1))
    sc = jnp.where(new_idx <= q_tok, sc, MASK_BIAS)
    _, l_fin, acc = update(carry, sc, vn)
    o_slc = acc / jnp.maximum(l_fin, 1.0)

    wl = win_ref.shape[1]
    kw = win_ref[0, :, 0:KV_WIDTH].astype(BF16)
    vw = win_ref[0, :, KV_WIDTH:2 * KV_WIDTH].astype(BF16)
    kwn = winn_ref[0, :, 0:KV_WIDTH]
    vwn = winn_ref[0, :, KV_WIDTH:2 * KV_WIDTH]
    w_pos = past_len - wl + lax.broadcasted_iota(jnp.int32, (1, wl), 1)
    dist = q_pos - w_pos
    mask_w = (w_pos >= 0) & (dist >= 0) & (dist < WINDOW)
    dist_n = q_tok - new_idx
    mask_n = (dist_n >= 0) & (dist_n < WINDOW)
    mask = jnp.concatenate([mask_w, mask_n], axis=1)
    sw = jnp.where(mask, jnp.concatenate([_dot_nt(q, kw), _dot_nt(q, kwn)], axis=1), NEG)
    mw = jnp.max(sw, axis=-1, keepdims=True)
    ew = jnp.where(mask, jnp.exp(sw - mw), 0.0)
    pw = (ew / jnp.maximum(jnp.sum(ew, axis=-1, keepdims=True), 1.0)).astype(BF16)
    o_win = _dot(pw[:, :wl], vw) + _dot(pw[:, wl:], vwn)

    gates = gate_ref[0]
    o_ref[0] = gates[:, 0:1] * o_cmp + gates[:, 1:2] * o_slc + gates[:, 2:3] * o_win


def _nsa_sample(nq_b, gates, kc, vc, lkv, nsakv_new, win_state, winkv_new, past_len):
    bs, ts, _ = nq_b.shape
    n_cp = kc.shape[2]
    n_cmp = (past_len + ts + CMP_STRIDE - 1) // CMP_STRIDE - 1
    n_slc = -(-(past_len + ts) // SLC_BLOCK)
    n_sp = -(-n_slc // LANES) * LANES
    n_top = min(SLC_TOP_N, n_slc)
    tk = min(2048, past_len)
    n_rows = N_HEADS_NSA * ts
    q5 = nq_b.reshape(bs, ts, NSA_KV_HEADS, NSA_GROUP, HEAD_DIM).transpose(0, 2, 3, 1, 4)
    zq = jnp.zeros_like(q5[:, 0])
    q_rows = jnp.stack([jnp.concatenate([q5[:, 0], zq], axis=-1), jnp.concatenate([zq, q5[:, 1]], axis=-1)], axis=1)
    q_rows = q_rows.reshape(bs, n_rows, KV_WIDTH)
    g_rows = gates.reshape(bs, ts, NSA_KV_HEADS, NSA_GROUP, 3).transpose(0, 2, 3, 1, 4).reshape(bs, n_rows, 3)
    g_rows = jnp.pad(g_rows, ((0, 0), (0, 0), (0, LANES - 3)))
    pad_new = lambda x: jnp.pad(x, ((0, 0), (0, LANES - ts), (0, 0))).astype(BF16)
    lkv_new = pad_new(nsakv_new[:, :, 2 * KV_WIDTH:])
    win_new = pad_new(winkv_new)
    cs = jnp.arange(n_cp) * CMP_STRIDE
    ss = jnp.arange(n_sp) * SLC_BLOCK
    ovl = ((cs[:, None] < ss[None, :] + SLC_BLOCK) & (cs[:, None] + CMP_BLOCK > ss[None, :])
           & (jnp.arange(n_cp)[:, None] < n_cmp) & (jnp.arange(n_sp)[None, :] < n_slc)).astype(BF16)
    wl = win_state.shape[1]
    per_b = lambda shape: pl.BlockSpec((1,) + shape, lambda b: (b,) + (0,) * len(shape))
    out = pl.pallas_call(
        functools.partial(_nsa_sample_kernel, ts=ts, past_len=past_len, n_cmp=n_cmp, n_slc=n_slc, n_top=n_top, tk=tk),
        grid=(bs,),
        in_specs=[per_b((n_rows, KV_WIDTH)), per_b((n_rows, LANES)),
                  per_b((NSA_KV_HEADS, n_cp, LANES)), per_b((NSA_KV_HEADS, n_cp, LANES)),
                  pl.BlockSpec((n_cp, n_sp), lambda b: (0, 0)),
                  per_b((past_len, 2 * KV_WIDTH)), per_b((LANES, 2 * KV_WIDTH)),
                  per_b((wl, 2 * KV_WIDTH)), per_b((LANES, 2 * KV_WIDTH))],
        out_specs=per_b((n_rows, KV_WIDTH)),
        out_shape=jax.ShapeDtypeStruct((bs, n_rows, KV_WIDTH), F32),
        compiler_params=_cparams(("arbitrary",)),
        name="nsa_sample",
    )(q_rows, g_rows, kc, vc, ovl, lkv, lkv_new, win_state, win_new)
    o5 = out.reshape(bs, NSA_KV_HEADS, NSA_GROUP, ts, NSA_KV_HEADS, HEAD_DIM)
    o = jnp.stack([o5[:, g, :, :, g] for g in range(NSA_KV_HEADS)], axis=1)
    return o.transpose(0, 3, 1, 2, 4).reshape(bs, ts, NSA_WIDTH)


def _rope(x, pos):
    half = HEAD_DIM // 2
    inv_freq = jnp.power(ROPE_THETA, -jnp.arange(half, dtype=F32) / half)
    ang = pos.astype(F32)[:, None] * inv_freq[None, :]
    cos = jnp.cos(ang)[None, :, None, :]
    sin = jnp.sin(ang)[None, :, None, :]
    x1, x2 = x[..., :half], x[..., half:]
    return jnp.concatenate([x1 * cos - x2 * sin, x2 * cos + x1 * sin], axis=-1)


def _masked_softmax(s, mask):
    s = jnp.where(mask, s, NEG)
    m = jnp.max(s, axis=-1, keepdims=True)
    e = jnp.where(mask, jnp.exp(s - m), 0.0)
    return e / jnp.maximum(jnp.sum(e, axis=-1, keepdims=True), 1.0)


def _sb_attend(q, q_pos, segments):
    scale = HEAD_DIM ** -0.5
    z = jnp.concatenate([jnp.einsum('bqhd,bkhd->bhqk', q, k).astype(F32) for k, _, _ in segments], axis=-1) * scale
    k_pos = jnp.concatenate([p for _, _, p in segments])
    mask = k_pos[None, :] < q_pos[:, None]
    log_keep = jnp.where(mask, jax.nn.log_sigmoid(-z), 0.0)
    log_stick = lax.cumsum(log_keep, axis=3, reverse=True) - log_keep
    a = jnp.where(mask, jnp.exp(jax.nn.log_sigmoid(z) + log_stick), 0.0)
    outs = []
    off = 0
    for k, v, _ in segments:
        n = k.shape[1]
        outs.append(jnp.einsum('bhqk,bkhd->bqhd', a[..., off:off + n], v))
        off += n
    return sum(outs[1:], outs[0])


def _compress_jax(rows, pe, w1, w2):
    b, t, g, d = rows.shape
    t_pad = -(-t // CMP_STRIDE) * CMP_STRIDE
    rows = jnp.pad(rows, ((0, 0), (0, t_pad - t), (0, 0), (0, 0)))
    n_chunk = t_pad // CMP_STRIDE
    ratio = CMP_BLOCK // CMP_STRIDE
    n_cmp = n_chunk - ratio + 1
    ch = rows.reshape(b, n_chunk, CMP_STRIDE, g, d)
    blocks = jnp.concatenate([ch[:, r:r + n_cmp] for r in range(ratio)], axis=2)
    blocks = blocks + pe[:, None, :]
    flat = blocks.transpose(0, 1, 3, 2, 4).reshape(b, n_cmp, g, CMP_BLOCK * d)
    out = jax.nn.silu(flat @ w1) @ w2
    end = jnp.arange(n_cmp) * CMP_STRIDE + CMP_BLOCK - 1
    return out, end


def _cmp_to_slc(n_cmp, n_slc):
    cs = jnp.arange(n_cmp) * CMP_STRIDE
    ss = jnp.arange(n_slc) * SLC_BLOCK
    return ((cs[:, None] < ss[None, :] + SLC_BLOCK) & (cs[:, None] + CMP_BLOCK > ss[None, :])).astype(F32)


def _nsa_attend(q, q_pos, kc, vc, c_end, ks, vs, kw, vw, w_pos, gates):
    b, tq = q.shape[:2]
    scale = HEAD_DIM ** -0.5
    qg = q.reshape(b, tq, NSA_KV_HEADS, NSA_GROUP, HEAD_DIM)
    s = jnp.einsum('bqgrd,bngd->bgrqn', qg, kc).astype(F32) * scale
    p_cmp = _masked_softmax(s, c_end[None, :] <= q_pos[:, None])
    o_cmp = jnp.einsum('bgrqn,bngd->bqgrd', p_cmp, vc)
    tk = ks.shape[1]
    n_slc = -(-tk // SLC_BLOCK)
    n_top = min(SLC_TOP_N, n_slc)
    imp = jnp.einsum('bgrqn,ns->bgqs', p_cmp, _cmp_to_slc(kc.shape[1], n_slc), precision=lax.Precision.HIGHEST)
    blk = jnp.arange(n_slc)[None, :]
    cur = (q_pos // SLC_BLOCK)[:, None]
    forced = (blk == 0) | (blk == cur) | (blk == cur - 1)
    imp = jnp.where(blk * SLC_BLOCK <= q_pos[:, None], imp + jnp.where(forced, FORCE_BONUS, 0.0), NEG)
    _, idx = lax.top_k(imp, n_top)
    pad = ((0, 0), (0, n_slc * SLC_BLOCK - tk), (0, 0), (0, 0))

    def to_blocks(t):
        return jnp.pad(t, pad).reshape(b, n_slc, SLC_BLOCK, NSA_KV_HEADS, HEAD_DIM).transpose(0, 3, 1, 2, 4)

    take = jax.vmap(jax.vmap(lambda blocks, ids: blocks[ids]))
    n_keys = n_top * SLC_BLOCK
    kg = take(to_blocks(ks), idx).reshape(b, NSA_KV_HEADS, tq, n_keys, HEAD_DIM)
    vg = take(to_blocks(vs), idx).reshape(b, NSA_KV_HEADS, tq, n_keys, HEAD_DIM)
    k_pos = (idx[..., None] * SLC_BLOCK + jnp.arange(SLC_BLOCK)).reshape(b, NSA_KV_HEADS, 1, tq, n_keys)
    s = jnp.einsum('bqgrd,bgqmd->bgrqm', qg, kg).astype(F32) * scale
    p = _masked_softmax(s, k_pos <= q_pos[:, None])
    o_slc = jnp.einsum('bgrqm,bgqmd->bqgrd', p, vg)
    s = jnp.einsum('bqgrd,bwgd->bgrqw', qg, kw).astype(F32) * scale
    dist = q_pos[:, None] - w_pos[None, :]
    p = _masked_softmax(s, (w_pos[None, :] >= 0) & (dist >= 0) & (dist < WINDOW))
    o_win = jnp.einsum('bgrqw,bwgd->bqgrd', p, vw)
    gt = gates.reshape(b, tq, NSA_KV_HEADS, NSA_GROUP, 3)
    o = gt[..., 0:1] * o_cmp + gt[..., 1:2] * o_slc + gt[..., 2:3] * o_win
    return o.reshape(b, tq, NSA_WIDTH)


def _gather_pages(pool, layer, page_table, slot):
    rows = pool[layer, page_table, :, slot]
    return rows.reshape(rows.shape[0], rows.shape[1] * rows.shape[2], rows.shape[3], rows.shape[4])


def _mix_out(x, o_sb, o_nsa, mods, lw, g_final):
    (gate_attn, shift_ffn, scale_ffn, gate_ffn) = mods
    (g_sb_out, g_nsa_out, w_out_b, g_ffn, w_router, b_router, w1_b, b_e1, w2_b, b_e2) = lw
    b, t, d = x.shape
    x1, h_b, logits = _outproj(x, o_sb, o_nsa, gate_attn, shift_ffn, scale_ffn, g_sb_out, g_nsa_out, g_ffn,
                               w_out_b, w_router, b_router)
    y = _moe(h_b.reshape(b * t, d), logits.reshape(b * t, LANES)[:, :N_EXPERTS], w1_b, b_e1, w2_b, b_e2)
    return _final(x1, y.reshape(b, t, d), gate_ffn, g_final)


def kernel(x_prompt, x_sample, c_prompt, c_sample, cache_sb_kv, cache_nsa_kv, state_win_kv, page_table, w_ada, b_ada, g_attn, w_in, g_sb_out, g_nsa_out, w_out, pe_k, w_ck1, w_ck2, pe_v, w_cv1, w_cv2, g_ffn, w_router, b_router, w_e1, b_e1, w_e2, b_e2, g_final):
    depth = w_ada.shape[0]
    assert depth == 1
    l = 0
    bp, t, d = x_prompt.shape
    bs, ts, _ = x_sample.shape

    w_all = _prep_w_in(w_in[l])
    w_out_b = w_out[l].astype(BF16)
    w1_b = w_e1[l].astype(BF16)
    w2_b = w_e2[l].astype(BF16)
    out_lw = (g_sb_out[l], g_nsa_out[l], w_out_b, g_ffn[l], w_router[l], b_router[l], w1_b, b_e1[l], w2_b, b_e2[l])

    n_c = bp + bs
    n_c_pad = -(-n_c // 8) * 8
    c_all = jnp.pad(jnp.concatenate([c_prompt, c_sample]), ((0, n_c_pad - n_c), (0, 0)))
    mod_all = _adaln(c_all, w_ada[l].astype(BF16), b_ada[l]).reshape(n_c_pad, 6, d)
    mp = [mod_all[:bp, i][:, None, :] for i in range(6)]
    ms = [jnp.repeat(mod_all[bp:n_c, i], ts, axis=0)[None] for i in range(6)]

    pos = jnp.arange(t)
    sq_b, sbkv, sbkv_b, nq_b, nsakv, winkv, dup_b, gates = _inproj(x_prompt, mp[0], mp[1], g_attn[l], w_all, pos)
    o_sb = _sb_prompt(sq_b, sbkv_b)
    rows4 = nsakv.reshape(bp, t // CMP_STRIDE, CMP_STRIDE, 4 * KV_WIDTH)
    kc = _compress(rows4, 0, pe_k[l], w_ck1[l], w_ck2[l], True)
    vc = _compress(rows4, 1, pe_v[l], w_cv1[l], w_cv2[l], False)
    o_nsa = _nsa_prompt(nq_b, gates, kc, vc, dup_b)
    y_prompt = _mix_out(x_prompt, o_sb, o_nsa, (mp[2], mp[3], mp[4], mp[5]), out_lw, g_final)
    keep = min(WINDOW, t)
    sb_kv_prompt = sbkv.reshape(1, bp, t, 2, N_HEADS_SB, HEAD_DIM)
    nsa_kv_prompt = nsakv.reshape(1, bp, t, 4, NSA_KV_HEADS, HEAD_DIM)
    win_kv_prompt = winkv[:, t - keep:].reshape(1, bp, keep, 2, NSA_KV_HEADS, HEAD_DIM)

    past_len = page_table.shape[1] * cache_sb_kv.shape[2]
    q_pos = past_len + jnp.arange(ts)
    pos_rows = jnp.tile(q_pos, bs)
    xs_flat = x_sample.reshape(1, bs * ts, d)
    s_sq_b, s_sbkv, s_sbkv_b, s_nq_b, s_nsakv, s_winkv, _, s_gates = _inproj(xs_flat, ms[0], ms[1], g_attn[l], w_all,
                                                                              pos_rows)
    per_seq = lambda a: a.reshape(bs, ts, a.shape[-1])
    n_pool, page_rows = cache_sb_kv.shape[1], cache_sb_kv.shape[2]
    sb_pool = cache_sb_kv[l].reshape(n_pool, page_rows, 2 * SB_WIDTH)
    nsa_pool = cache_nsa_kv[l].reshape(n_pool, page_rows, 4 * KV_WIDTH)
    o_sb_s = _sb_sample(per_seq(s_sq_b), per_seq(s_sbkv_b), sb_pool, page_table)
    nsakv_s = per_seq(s_nsakv)
    winkv_s = per_seq(s_winkv)
    new_cmp = jnp.pad(nsakv_s[:, :, :2 * KV_WIDTH], ((0, 0), (0, CMP_STRIDE - ts), (0, 0)))
    kc_s, vc_s, lkv = _nsa_stage(nsa_pool, page_table, new_cmp, (pe_k[l], w_ck1[l], w_ck2[l]),
                                 (pe_v[l], w_cv1[l], w_cv2[l]))
    win_state = state_win_kv[l].reshape(bs, -1, 2 * KV_WIDTH)
    o_nsa_s = _nsa_sample(per_seq(s_nq_b), per_seq(s_gates)[:, :, :N_GATES], kc_s, vc_s, lkv, nsakv_s, win_state,
                          winkv_s, past_len)
    y_sample = _mix_out(xs_flat, o_sb_s.reshape(1, bs * ts, SB_WIDTH), o_nsa_s.reshape(1, bs * ts, NSA_WIDTH),
                        (ms[2], ms[3], ms[4], ms[5]), out_lw, g_final).reshape(bs, ts, d)
    sb_kv_sample = per_seq(s_sbkv).reshape(1, bs, ts, 2, N_HEADS_SB, HEAD_DIM)
    nsa_kv_sample = nsakv_s.reshape(1, bs, ts, 4, NSA_KV_HEADS, HEAD_DIM)
    win_kv_sample = jnp.concatenate([win_state, winkv_s], axis=1)[:, ts:].reshape(1, bs, -1, 2, NSA_KV_HEADS, HEAD_DIM)
    return (y_prompt, y_sample, sb_kv_prompt, nsa_kv_prompt, win_kv_prompt, sb_kv_sample, nsa_kv_sample, win_kv_sample)
```

```python
import functools

import jax
import jax.numpy as jnp
from jax import lax
from jax.experimental import pallas as pl
from jax.experimental.pallas import tpu as pltpu

HEAD_DIM = 64
N_HEADS_SB = 8
N_HEADS_NSA = 8
NSA_KV_HEADS = 2
NSA_GROUP = N_HEADS_NSA // NSA_KV_HEADS
SB_WIDTH = N_HEADS_SB * HEAD_DIM
NSA_WIDTH = N_HEADS_NSA * HEAD_DIM
KV_WIDTH = NSA_KV_HEADS * HEAD_DIM
N_GATES = 3 * N_HEADS_NSA
CMP_BLOCK = 32
CMP_STRIDE = 16
SLC_BLOCK = 64
SLC_TOP_N = 16
WINDOW = 512
N_EXPERTS = 32
TOP_K = 4
SWIGLU_LIMIT = 7.0
SWIGLU_ALPHA = 1.702
MOE_ROW_BLOCK = 256
ROPE_THETA = 10000.0
EPS = 1e-6
NEG = -1e30
FORCE_BONUS = 1e3
MASK_BIAS = -(2.0 ** 30)

LANES = 128
VMEM_LIMIT = 56 * 1024 * 1024

BF16 = jnp.bfloat16
F32 = jnp.float32


def _cparams(sem):
    return pltpu.CompilerParams(dimension_semantics=sem, vmem_limit_bytes=VMEM_LIMIT)


def _dot(a, b):
    return jnp.dot(a, b, preferred_element_type=F32)


def _dot_nt(a, b):
    return lax.dot_general(a, b, (((1,), (1,)), ((), ())), preferred_element_type=F32)


def _split2(x):
    hi = x.astype(BF16)
    lo = (x - hi.astype(F32)).astype(BF16)
    return hi, lo


def _rms(x, g):
    return (x * lax.rsqrt(jnp.mean(x * x, axis=-1, keepdims=True) + EPS)) * g


def _rope_slab(x, cos, sin_signed):
    lane = lax.broadcasted_iota(jnp.int32, x.shape, 1)
    first = (lane % HEAD_DIM) < (HEAD_DIM // 2)
    partner = jnp.where(first, pltpu.roll(x, LANES - HEAD_DIM // 2, 1), pltpu.roll(x, HEAD_DIM // 2, 1))
    return x * cos + partner * sin_signed


def _adaln_kernel(c_ref, w_ref, b_ref, o_ref):
    c = c_ref[...]
    s = (c * jax.nn.sigmoid(c)).astype(BF16)
    o_ref[...] = _dot(s, w_ref[...]) + b_ref[...]


def _adaln(c, w_ada_b, b_ada):
    r, d = c.shape
    n = w_ada_b.shape[1]
    tn = d
    return pl.pallas_call(
        _adaln_kernel,
        grid=(n // tn,),
        in_specs=[pl.BlockSpec((r, d), lambda j: (0, 0)),
                  pl.BlockSpec((d, tn), lambda j: (0, j)),
                  pl.BlockSpec((1, tn), lambda j: (0, j))],
        out_specs=pl.BlockSpec((r, tn), lambda j: (0, j)),
        out_shape=jax.ShapeDtypeStruct((r, n), F32),
        compiler_params=_cparams(("arbitrary",)),
        name="adaln",
    )(c, w_ada_b, b_ada.reshape(1, n))


_C_SQ = 0
_C_SBKV = SB_WIDTH
_C_NQ = 3 * SB_WIDTH
_C_NSAKV = _C_NQ + NSA_WIDTH
_C_WINKV = _C_NSAKV + 4 * KV_WIDTH
_C_DUP = _C_WINKV + 2 * KV_WIDTH
_C_GATE = _C_DUP + 8 * LANES
_C_END = _C_GATE + LANES


def _prep_w_in(w_in):
    main = w_in[:, :_C_DUP]
    dups = []
    for g in range(NSA_KV_HEADS):
        for slot in (2, 3, 4, 5):
            c0 = _C_NSAKV + slot * KV_WIDTH + g * HEAD_DIM
            col = w_in[:, c0:c0 + HEAD_DIM]
            dups += [col, col]
    gates = jnp.pad(w_in[:, _C_DUP:_C_DUP + N_GATES], ((0, 0), (0, LANES - N_GATES)))
    return jnp.concatenate([main] + dups + [gates], axis=1).astype(BF16)


def _inproj_kernel(x_ref, shift_ref, scale_ref, g_ref, w_ref, cos_ref, sin_ref,
                   sq_ref, sbkv_ref, sbkvb_ref, nq_ref, nsakv_ref, winkv_ref, dup_ref, gate_ref):
    x = x_ref[0]
    h = _rms(x, g_ref[...]) * (1.0 + scale_ref[0]) + shift_ref[0]
    hb = h.astype(BF16)
    cos = cos_ref[...]
    sin = sin_ref[...]
    qscale = HEAD_DIM ** -0.5

    def proj(c0, width):
        return _dot(hb, w_ref[:, c0:c0 + width])

    sq_ref[0] = (proj(_C_SQ, SB_WIDTH) * qscale).astype(BF16)
    kv = proj(_C_SBKV, 2 * SB_WIDTH)
    sbkv_ref[0] = kv
    sbkvb_ref[0] = kv.astype(BF16)
    for p in range(NSA_WIDTH // LANES):
        y = _rope_slab(proj(_C_NQ + p * LANES, LANES), cos, sin)
        nq_ref[0, :, p * LANES:(p + 1) * LANES] = (y * qscale).astype(BF16)
    for s in range(4):
        y = proj(_C_NSAKV + s * LANES, LANES)
        if s == 2:
            y = _rope_slab(y, cos, sin)
        nsakv_ref[0, :, s * LANES:(s + 1) * LANES] = y
    for s in range(2):
        y = proj(_C_WINKV + s * LANES, LANES)
        if s == 0:
            y = _rope_slab(y, cos, sin)
        winkv_ref[0, :, s * LANES:(s + 1) * LANES] = y
    for s in range(8):
        y = proj(_C_DUP + s * LANES, LANES)
        if s % 2 == 0:
            y = _rope_slab(y, cos, sin)
        dup_ref[0, :, s * LANES:(s + 1) * LANES] = y.astype(BF16)
    gate_ref[0] = jax.nn.sigmoid(proj(_C_GATE, LANES))


def _rope_tables(pos):
    half = HEAD_DIM // 2
    inv_freq = jnp.power(ROPE_THETA, -jnp.arange(half, dtype=F32) / half)
    ang = pos.astype(F32)[:, None] * inv_freq[None, :]
    cos, sin = jnp.cos(ang), jnp.sin(ang)
    cos_t = jnp.concatenate([cos, cos, cos, cos], axis=1)
    sin_t = jnp.concatenate([-sin, sin, -sin, sin], axis=1)
    return cos_t, sin_t


def _inproj(x, shift, scale, g_attn, w_all, pos):
    b, t, d = x.shape
    tm = min(256, t)
    tmod = shift.shape[1]
    cos_t, sin_t = _rope_tables(pos)
    if tmod == 1:
        mod_spec = pl.BlockSpec((1, 1, d), lambda i, j: (i, 0, 0))
    else:
        mod_spec = pl.BlockSpec((1, tm, d), lambda i, j: (i, j, 0))
    widths = (SB_WIDTH, 2 * SB_WIDTH, 2 * SB_WIDTH, NSA_WIDTH, 4 * KV_WIDTH, 2 * KV_WIDTH, 8 * LANES, LANES)
    dtypes = (BF16, F32, BF16, BF16, F32, F32, BF16, F32)
    return pl.pallas_call(
        _inproj_kernel,
        grid=(b, t // tm),
        in_specs=[pl.BlockSpec((1, tm, d), lambda i, j: (i, j, 0)),
                  mod_spec, mod_spec,
                  pl.BlockSpec((1, d), lambda i, j: (0, 0)),
                  pl.BlockSpec((d, _C_END), lambda i, j: (0, 0)),
                  pl.BlockSpec((tm, LANES), lambda i, j: (j, 0)),
                  pl.BlockSpec((tm, LANES), lambda i, j: (j, 0))],
        out_specs=[pl.BlockSpec((1, tm, w), lambda i, j: (i, j, 0)) for w in widths],
        out_shape=[jax.ShapeDtypeStruct((b, t, w), dt) for w, dt in zip(widths, dtypes)],
        compiler_params=_cparams(("arbitrary", "arbitrary")),
        name="inproj",
    )(x, shift, scale, g_attn.reshape(1, d), w_all, cos_t, sin_t)


def _sb_scores(q_h, k, tri, mask):
    z = _dot_nt(q_h, k)
    neg_log_keep = jnp.maximum(z, 0.0) + jnp.log(1.0 + jnp.exp(-jnp.abs(z)))
    if mask is not None:
        neg_log_keep = jnp.where(mask, neg_log_keep, 0.0)
    hi, lo = _split2(neg_log_keep)
    later = _dot(hi, tri) + _dot(lo, tri)
    return z - neg_log_keep - later, jnp.sum(neg_log_keep, axis=-1, keepdims=True)


def _sb_kernel(q_ref, k_ref, v_ref, o_ref, *, tq):
    qi = pl.program_id(2)
    q = q_ref[0]
    lane = lax.broadcasted_iota(jnp.int32, (1, LANES), 1)
    head_lanes = [lane < HEAD_DIM, lane >= HEAD_DIM]
    zero = jnp.zeros((), BF16)
    q_heads = [jnp.where(m, q, zero) for m in head_lanes]
    row = lax.broadcasted_iota(jnp.int32, (tq, tq), 0)
    col = lax.broadcasted_iota(jnp.int32, (tq, tq), 1)
    tri = (row > col).astype(BF16)
    diag_mask = col < row

    def tiles(js, carry, mask):
        rs = list(carry[:2])
        acc = carry[2]
        parts = []
        for j in js:
            start = pl.multiple_of(j * tq, tq)
            k = k_ref[0, pl.ds(start, tq), :]
            v = v_ref[0, pl.ds(start, tq), :]
            parts.append([(_sb_scores(q_heads[hh], k, tri, mask), jnp.where(head_lanes[hh], v, zero))
                          for hh in range(2)])
        for per_head in parts:
            for hh in range(2):
                (expo, total), v_h = per_head[hh]
                a = jnp.exp(expo - rs[hh])
                if mask is not None:
                    a = jnp.where(mask, a, 0.0)
                acc = acc + _dot(a.astype(BF16), v_h)
                rs[hh] = rs[hh] + total
        return rs[0], rs[1], acc

    init = (jnp.zeros((tq, 1), F32), jnp.zeros((tq, 1), F32), jnp.zeros((tq, LANES), F32))
    carry = tiles([qi], init, diag_mask)
    carry = lax.cond(qi % 2 == 1, lambda c: tiles([qi - 1], c, None), lambda c: c, carry)
    top = qi - qi % 2
    carry = lax.fori_loop(0, top // 2, lambda s, c: tiles([top - 1 - 2 * s, top - 2 - 2 * s], c, None), carry)
    o_ref[0] = carry[2]


def _sb_prompt(sq_b, sbkv_b):
    b, t, _ = sq_b.shape
    tq = min(256, t)
    n_pair = SB_WIDTH // LANES
    return pl.pallas_call(
        functools.partial(_sb_kernel, tq=tq),
        grid=(b, n_pair, t // tq),
        in_specs=[pl.BlockSpec((1, tq, LANES), lambda i, p, j: (i, j, p)),
                  pl.BlockSpec((1, t, LANES), lambda i, p, j: (i, 0, p)),
                  pl.BlockSpec((1, t, LANES), lambda i, p, j: (i, 0, n_pair + p))],
        out_specs=pl.BlockSpec((1, tq, LANES), lambda i, p, j: (i, j, p)),
        out_shape=jax.ShapeDtypeStruct((b, t, SB_WIDTH), F32),
        compiler_params=_cparams(("arbitrary", "arbitrary", "arbitrary")),
        name="sb_prompt",
    )(sq_b, sbkv_b, sbkv_b)


def _prep_cmp_weights(pe, w1, w2):
    hid = w1.shape[1]
    w1r = w1.reshape(CMP_BLOCK, HEAD_DIM, hid)
    z = jnp.zeros_like(w1r)
    w1_bd = jnp.concatenate([jnp.concatenate([w1r, z], axis=2), jnp.concatenate([z, w1r], axis=2)], axis=1)
    w2d = jnp.concatenate([w2, w2], axis=1)
    z2 = jnp.zeros_like(w2d)
    w2_bd = jnp.concatenate([jnp.concatenate([w2d, z2], axis=1), jnp.concatenate([z2, w2d], axis=1)], axis=0)
    pe_d = jnp.concatenate([pe, pe], axis=1)
    w1_pair = w1_bd.reshape(CMP_BLOCK // 2, 2 * LANES, 2 * hid)
    return pe_d, w1_pair.astype(BF16), w2_bd.astype(BF16)


def _compress_mlp(load_row, n, pe_ref, w1_ref, w2_ref, bot_ref, cos_sin, store):
    hid2 = w1_ref.shape[2]
    half = CMP_STRIDE // 2
    top = jnp.zeros((n, hid2), F32)
    bot = jnp.zeros((n, hid2), F32)
    for rp in range(half):
        xa = load_row(2 * rp)
        xb = load_row(2 * rp + 1)

        def lhs(off):
            r0 = off + 2 * rp
            return jnp.concatenate([xa + pe_ref[r0:r0 + 1, :], xb + pe_ref[r0 + 1:r0 + 2, :]], axis=1).astype(BF16)

        top = top + _dot(lhs(0), w1_ref[rp])
        bot = bot + _dot(lhs(CMP_STRIDE), w1_ref[half + rp])
    bot_ref[pl.ds(0, n), :] = bot
    bot_ref[pl.ds(n, 8), :] = jnp.zeros((8, hid2), F32)
    hidden = top + bot_ref[pl.ds(1, n), :]
    hidden = hidden * jax.nn.sigmoid(hidden)
    out = _dot(hidden.astype(BF16), w2_ref[...])
    for g in range(NSA_KV_HEADS):
        y = out[:, g * LANES:(g + 1) * LANES]
        if cos_sin is not None:
            y = _rope_slab(y, cos_sin[0][...], cos_sin[1][...])
        store(g, y.astype(BF16))


def _compress_kernel(x_ref, pe_ref, w1_ref, w2_ref, cos_ref, sin_ref, o_ref, bot_ref, *, rope):
    n = x_ref.shape[1] // CMP_STRIDE

    def store(g, y):
        o_ref[0, g] = y

    _compress_mlp(lambda r: x_ref[0, pl.ds(r, n, stride=CMP_STRIDE), :], n, pe_ref, w1_ref, w2_ref, bot_ref,
                  (cos_ref, sin_ref) if rope else None, store)


def _compress(rows4, slot, pe, w1, w2, rope):
    b, t, _ = rows4.shape
    n = t // CMP_STRIDE
    pe_d, w1_bd, w2_bd = _prep_cmp_weights(pe, w1, w2)
    c_end = jnp.arange(n) * CMP_STRIDE + CMP_BLOCK - 1
    cos_t, sin_t = _rope_tables(c_end)
    hid2 = w1_bd.shape[2]
    return pl.pallas_call(
        functools.partial(_compress_kernel, rope=rope),
        grid=(b,),
        in_specs=[pl.BlockSpec((1, t, LANES), lambda i: (i, 0, slot)),
                  pl.BlockSpec((CMP_BLOCK, LANES), lambda i: (0, 0)),
                  pl.BlockSpec((CMP_BLOCK // 2, 2 * LANES, hid2), lambda i: (0, 0, 0)),
                  pl.BlockSpec((hid2, 2 * LANES), lambda i: (0, 0)),
                  pl.BlockSpec((n, LANES), lambda i: (0, 0)),
                  pl.BlockSpec((n, LANES), lambda i: (0, 0))],
        out_specs=pl.BlockSpec((1, NSA_KV_HEADS, n, LANES), lambda i: (i, 0, 0, 0)),
        out_shape=jax.ShapeDtypeStruct((b, NSA_KV_HEADS, n, LANES), BF16),
        scratch_shapes=[pltpu.VMEM((n + 8, hid2), F32)],
        compiler_params=_cparams(("arbitrary",)),
        name="compress",
    )(rows4, pe_d, w1_bd, w2_bd, cos_t, sin_t)


def _nsa_kernel(q_ref, gate_ref, kc_ref, vc_ref, ovl_ref, ks_ref, vs_ref, kw_ref, vw_ref, o_ref, *, tq, tk, n_top):
    g = pl.program_id(1)
    qi = pl.program_id(2)
    t_len = ks_ref.shape[2]
    n_c = kc_ref.shape[2]
    n_slc = ovl_ref.shape[0]
    s0 = qi * tq
    zero = jnp.zeros((), BF16)
    lane = lax.broadcasted_iota(jnp.int32, (1, LANES), 1)
    head_lanes = [lane < HEAD_DIM, lane >= HEAD_DIM]
    q = q_ref[0]
    q_rows = jnp.concatenate(
        [jnp.where(head_lanes[h % 2], q[:, (h // 2) * LANES:(h // 2 + 1) * LANES], zero) for h in range(NSA_GROUP)],
        axis=0)
    q_pos = s0 + lax.broadcasted_iota(jnp.int32, (tq, 1), 0)
    q_pos4 = jnp.concatenate([q_pos] * NSA_GROUP, axis=0)

    def head_merge(o_rows):
        slabs = []
        for p in range(NSA_GROUP // 2):
            a = o_rows[(2 * p) * tq:(2 * p + 1) * tq]
            b2 = o_rows[(2 * p + 1) * tq:(2 * p + 2) * tq]
            slabs.append(jnp.where(head_lanes[0], a, b2))
        return jnp.concatenate(slabs, axis=1)

    kc = kc_ref[0, 0]
    vc = vc_ref[0, 0]
    s = _dot_nt(q_rows, kc)
    c_idx = lax.broadcasted_iota(jnp.int32, (1, n_c), 1)
    c_end = c_idx * CMP_STRIDE + (CMP_BLOCK - 1)
    c_mask = (c_end <= q_pos4) & (c_idx < n_c - 1)
    s = jnp.where(c_mask, s, NEG)
    m = jnp.max(s, axis=-1, keepdims=True)
    e = jnp.where(c_mask, jnp.exp(s - m), 0.0)
    p_cmp = e / jnp.maximum(jnp.sum(e, axis=-1, keepdims=True), 1.0)
    o_cmp = head_merge(_dot(p_cmp.astype(BF16), vc))

    p_sum = p_cmp[0:tq]
    for h in range(1, NSA_GROUP):
        p_sum = p_sum + p_cmp[h * tq:(h + 1) * tq]
    ovl = ovl_ref[...]
    p_hi, p_lo = _split2(p_sum)
    p_lo2 = (p_sum - p_hi.astype(F32) - p_lo.astype(F32)).astype(BF16)
    imp = _dot_nt(ovl, p_hi) + _dot_nt(ovl, p_lo) + _dot_nt(ovl, p_lo2)
    blk = lax.broadcasted_iota(jnp.int32, (n_slc, 1), 0)
    qp_l = s0 + lax.broadcasted_iota(jnp.int32, (1, tq), 1)
    cur = qp_l // SLC_BLOCK
    forced = (blk == 0) | (blk == cur) | (blk == cur - 1)
    imp = jnp.where(blk * SLC_BLOCK <= qp_l, imp + jnp.where(forced, FORCE_BONUS, 0.0), NEG)
    blk_full = lax.broadcasted_iota(jnp.int32, (n_slc, tq), 0)

    def pick(_, carry):
        imp_c, unsel = carry
        best = jnp.max(imp_c, axis=0, keepdims=True)
        first = jnp.min(jnp.where(imp_c == best, blk_full, n_slc), axis=0, keepdims=True)
        hit = blk_full == first
        return jnp.where(hit, -jnp.inf, imp_c), jnp.where(hit, 0.0, unsel)

    _, unsel_t = lax.fori_loop(0, n_top, pick, (imp, jnp.ones((n_slc, tq), F32)))
    unsel = unsel_t.T.astype(BF16)
    q_aug = jnp.concatenate([q_rows, jnp.concatenate([unsel] * NSA_GROUP, axis=0)], axis=1)

    blocks_per_tile = tk // SLC_BLOCK
    k_row_blk = lax.broadcasted_iota(jnp.int32, (tk, n_slc), 0) // SLC_BLOCK
    k_lane = lax.broadcasted_iota(jnp.int32, (tk, n_slc), 1)
    k_off = lax.broadcasted_iota(jnp.int32, (1, tk), 1)

    def slc_tile(j, carry, causal):
        m_run, l_run, acc = carry
        start = pl.multiple_of(j * tk, tk)
        k = ks_ref[0, 0, pl.ds(start, tk), :]
        v = vs_ref[0, 0, pl.ds(start, tk), :]
        onehot = jnp.where(k_lane == k_row_blk + j * blocks_per_tile, MASK_BIAS, 0.0).astype(BF16)
        k_aug = jnp.concatenate([k, onehot], axis=1)
        sc = _dot_nt(q_aug, k_aug)
        if causal:
            sc = jnp.where(start + k_off <= q_pos4, sc, MASK_BIAS)
        m_new = jnp.maximum(m_run, jnp.max(sc, axis=-1, keepdims=True))
        alpha = jnp.exp(m_run - m_new)
        pr = jnp.exp(sc - m_new)
        l_new = alpha * l_run + jnp.sum(pr, axis=-1, keepdims=True)
        acc = alpha * acc + _dot(pr.astype(BF16), v)
        return m_new, l_new, acc

    last = (s0 + tq - 1) // tk
    init = (jnp.full((NSA_GROUP * tq, 1), NEG, F32), jnp.zeros((NSA_GROUP * tq, 1), F32),
            jnp.zeros((NSA_GROUP * tq, LANES), F32))
    carry = lax.fori_loop(0, last, lambda j, c: slc_tile(j, c, False), init)
    _, l_fin, acc = slc_tile(last, carry, True)
    o_slc = head_merge(acc / jnp.maximum(l_fin, 1.0))

    band = min(WINDOW + tq, t_len)
    w_start = jnp.clip(s0 - WINDOW, 0, t_len - band)
    w_start = pl.multiple_of(w_start, tq)
    kw = kw_ref[0, 0, pl.ds(w_start, band), :]
    vw = vw_ref[0, 0, pl.ds(w_start, band), :]
    sw = _dot_nt(q_rows, kw)
    w_pos = w_start + lax.broadcasted_iota(jnp.int32, (1, band), 1)
    dist = q_pos4 - w_pos
    w_mask = (dist >= 0) & (dist < WINDOW)
    sw = jnp.where(w_mask, sw, NEG)
    mw = jnp.max(sw, axis=-1, keepdims=True)
    ew = jnp.where(w_mask, jnp.exp(sw - mw), 0.0)
    pw = ew / jnp.maximum(jnp.sum(ew, axis=-1, keepdims=True), 1.0)
    o_win = head_merge(_dot(pw.astype(BF16), vw))

    gates = gate_ref[0]
    g_hi, g_lo = _split2(gates)
    col = lax.broadcasted_iota(jnp.int32, (LANES, NSA_GROUP * HEAD_DIM), 0)
    head_of_lane = lax.broadcasted_iota(jnp.int32, (LANES, NSA_GROUP * HEAD_DIM), 1) // HEAD_DIM + g * NSA_GROUP
    out = jnp.zeros((tq, NSA_GROUP * HEAD_DIM), F32)
    for branch, o_b in enumerate((o_cmp, o_slc, o_win)):
        expand = (col == head_of_lane * 3 + branch).astype(BF16)
        out = out + (_dot(g_hi, expand) + _dot(g_lo, expand)) * o_b
    o_ref[0] = out


def _cmp_overlap_t(n_c, n_slc):
    cs = jnp.arange(n_c) * CMP_STRIDE
    ss = jnp.arange(n_slc) * SLC_BLOCK
    ok = (cs[None, :] < ss[:, None] + SLC_BLOCK) & (cs[None, :] + CMP_BLOCK > ss[:, None]) & (jnp.arange(n_c)[None, :] < n_c - 1)
    return ok.astype(BF16)


def _nsa_prompt(nq_b, gates, kc, vc, dup_b):
    b, t, _ = nq_b.shape
    n_c = kc.shape[2]
    n_slc = t // SLC_BLOCK
    tq = 128
    tk = min(512, t)
    n_top = min(SLC_TOP_N, n_slc)
    gw = NSA_GROUP * HEAD_DIM
    kv_spec = lambda slot: pl.BlockSpec((1, 1, t, LANES), lambda i, g, j: (i, 0, 0, g * 4 + slot))
    dup4 = dup_b.reshape(b, 1, t, 8 * LANES)
    return pl.pallas_call(
        functools.partial(_nsa_kernel, tq=tq, tk=tk, n_top=n_top),
        grid=(b, NSA_KV_HEADS, t // tq),
        in_specs=[pl.BlockSpec((1, tq, gw), lambda i, g, j: (i, j, g)),
                  pl.BlockSpec((1, tq, LANES), lambda i, g, j: (i, j, 0)),
                  pl.BlockSpec((1, 1, n_c, LANES), lambda i, g, j: (i, g, 0, 0)),
                  pl.BlockSpec((1, 1, n_c, LANES), lambda i, g, j: (i, g, 0, 0)),
                  pl.BlockSpec((n_slc, n_c), lambda i, g, j: (0, 0)),
                  kv_spec(0), kv_spec(1), kv_spec(2), kv_spec(3)],
        out_specs=pl.BlockSpec((1, tq, gw), lambda i, g, j: (i, j, g)),
        out_shape=jax.ShapeDtypeStruct((b, t, NSA_WIDTH), F32),
        compiler_params=_cparams(("arbitrary", "arbitrary", "arbitrary")),
        name="nsa_prompt",
    )(nq_b, gates, kc, vc, _cmp_overlap_t(n_c, n_slc), dup4, dup4, dup4, dup4)


def _outproj_kernel(x_ref, osb_ref, onsa_ref, gate_ref, shift_ref, scale_ref, gsb_ref, gnsa_ref, gffn_ref,
                    wo_ref, wr_ref, br_ref, x1_ref, h_ref, logit_ref):
    o_sb = _rms(osb_ref[0], gsb_ref[...]).astype(BF16)
    o_nsa = _rms(onsa_ref[0], gnsa_ref[...]).astype(BF16)
    mixed = _dot(o_sb, wo_ref[0:SB_WIDTH, :]) + _dot(o_nsa, wo_ref[SB_WIDTH:, :])
    x1 = x_ref[0] + gate_ref[0] * mixed
    x1_ref[0] = x1
    h = _rms(x1, gffn_ref[...]) * (1.0 + scale_ref[0]) + shift_ref[0]
    h_ref[0] = h.astype(BF16)
    h_hi, h_lo = _split2(h)
    logit_ref[0] = _dot(h_hi, wr_ref[0]) + _dot(h_lo, wr_ref[0]) + _dot(h_hi, wr_ref[1]) + br_ref[...]


def _outproj(x, o_sb, o_nsa, gate, shift, scale, g_sb, g_nsa, g_ffn, w_out_b, w_router, b_router):
    b, t, d = x.shape
    tm = min(256, t)
    tmod = gate.shape[1]
    if tmod == 1:
        mod_spec = pl.BlockSpec((1, 1, d), lambda i, j: (i, 0, 0))
    else:
        mod_spec = pl.BlockSpec((1, tm, d), lambda i, j: (i, j, 0))
    wr = jnp.pad(w_router, ((0, 0), (0, LANES - N_EXPERTS)))
    wr_hi = wr.astype(BF16)
    wr_lo = (wr - wr_hi.astype(F32)).astype(BF16)
    wr2 = jnp.stack([wr_hi, wr_lo])
    br = jnp.pad(b_router, (0, LANES - N_EXPERTS)).reshape(1, LANES)
    row = lambda w: pl.BlockSpec((1, tm, w), lambda i, j: (i, j, 0))
    vec = lambda w: pl.BlockSpec((1, w), lambda i, j: (0, 0))
    return pl.pallas_call(
        _outproj_kernel,
        grid=(b, t // tm),
        in_specs=[row(d), row(SB_WIDTH), row(NSA_WIDTH), mod_spec, mod_spec, mod_spec,
                  vec(SB_WIDTH), vec(NSA_WIDTH), vec(d),
                  pl.BlockSpec((SB_WIDTH + NSA_WIDTH, d), lambda i, j: (0, 0)),
                  pl.BlockSpec((2, d, LANES), lambda i, j: (0, 0, 0)),
                  vec(LANES)],
        out_specs=[row(d), row(d), row(LANES)],
        out_shape=[jax.ShapeDtypeStruct((b, t, d), F32), jax.ShapeDtypeStruct((b, t, d), BF16),
                   jax.ShapeDtypeStruct((b, t, LANES), F32)],
        compiler_params=_cparams(("arbitrary", "arbitrary")),
        name="outproj",
    )(x, o_sb, o_nsa, gate, shift, scale, g_sb.reshape(1, -1), g_nsa.reshape(1, -1), g_ffn.reshape(1, -1),
      w_out_b, wr2, br)


def _moe_kernel(be_ref, xs_ref, gate_ref, w1_ref, b1_ref, w2_ref, b2_ref, o_ref):
    del be_ref
    d_ff = w2_ref.shape[1]
    gu = _dot(xs_ref[...], w1_ref[0]) + b1_ref[0]
    gl = jnp.minimum(gu[:, :d_ff], SWIGLU_LIMIT)
    u = jnp.clip(gu[:, d_ff:], -SWIGLU_LIMIT, SWIGLU_LIMIT)
    act = (u + 1.0) * gl * jax.nn.sigmoid(SWIGLU_ALPHA * gl)
    y = _dot(act.astype(BF16), w2_ref[0]) + b2_ref[0]
    o_ref[...] = y * gate_ref[...]


def _moe_ffn(xs, row_gate, block_expert, w1_b, b1, w2_b, b2, rb):
    rows, d = xs.shape
    n_blocks = rows // rb
    e, _, f2 = w1_b.shape
    grid_spec = pltpu.PrefetchScalarGridSpec(
        num_scalar_prefetch=1,
        grid=(n_blocks,),
        in_specs=[pl.BlockSpec((rb, d), lambda i, be: (i, 0)),
                  pl.BlockSpec((rb, 1), lambda i, be: (i, 0)),
                  pl.BlockSpec((1, d, f2), lambda i, be: (be[i], 0, 0)),
                  pl.BlockSpec((1, 1, f2), lambda i, be: (be[i], 0, 0)),
                  pl.BlockSpec((1, f2 // 2, d), lambda i, be: (be[i], 0, 0)),
                  pl.BlockSpec((1, 1, d), lambda i, be: (be[i], 0, 0))],
        out_specs=pl.BlockSpec((rb, d), lambda i, be: (i, 0)),
    )
    return pl.pallas_call(
        _moe_kernel,
        grid_spec=grid_spec,
        out_shape=jax.ShapeDtypeStruct((rows, d), F32),
        compiler_params=_cparams(("arbitrary",)),
        name="moe_ffn",
    )(block_expert, xs, row_gate.reshape(rows, 1), w1_b, b1.reshape(e, 1, f2), w2_b, b2.reshape(e, 1, d))


def _moe(h_b, logits, w1_b, b1, w2_b, b2):
    n, d = h_b.shape
    a = n * TOP_K
    top_val, top_idx = lax.top_k(logits, TOP_K)
    gate = jax.nn.softmax(top_val, axis=-1)
    rb = MOE_ROW_BLOCK if a >= MOE_ROW_BLOCK * N_EXPERTS else max(8, a // N_EXPERTS)
    n_blocks = -(-a // rb) + N_EXPERTS
    iota = jnp.arange(a, dtype=jnp.int32)
    e_sorted, order, g_sorted = lax.sort((top_idx.reshape(a).astype(jnp.int32), iota, gate.reshape(a)), num_keys=1)
    experts = jnp.arange(N_EXPERTS, dtype=jnp.int32)
    start = jnp.searchsorted(e_sorted, experts, side='left').astype(jnp.int32)
    counts = jnp.searchsorted(e_sorted, experts, side='right').astype(jnp.int32) - start
    padded = (counts + rb - 1) // rb * rb
    pad_end = jnp.cumsum(padded)
    pad_start = pad_end - padded
    block_expert = jnp.minimum(jnp.searchsorted(pad_end, jnp.arange(n_blocks) * rb, side='right'),
                               N_EXPERTS - 1).astype(jnp.int32)
    within = (jnp.arange(n_blocks, dtype=jnp.int32) * rb - pad_start[block_expert])[:, None] + jnp.arange(rb, dtype=jnp.int32)
    valid = within < counts[block_expert][:, None]
    src = jnp.where(valid, start[block_expert][:, None] + within, 0).reshape(-1)
    valid = valid.reshape(-1)
    row_tok = jnp.where(valid, order[src] // TOP_K, n)
    row_gate = jnp.where(valid, g_sorted[src], 0.0)
    xs = jnp.concatenate([h_b, jnp.zeros((1, d), h_b.dtype)])[row_tok]
    ys = _moe_ffn(xs, row_gate, block_expert, w1_b, b1, w2_b, b2, rb)
    shift = pad_start - start
    step = shift - jnp.concatenate([jnp.zeros((1,), shift.dtype), shift[:-1]])
    dest_sorted = iota + jnp.sum(jnp.where(iota[:, None] >= start[None, :], step[None, :], 0), axis=1).astype(jnp.int32)
    _, dest = lax.sort((order, dest_sorted), num_keys=1)
    return ys[dest].reshape(n, TOP_K * d)


def _final_kernel(x1_ref, y_ref, gate_ref, g_ref, o_ref):
    d = x1_ref.shape[2]
    y = y_ref[0, :, 0:d]
    for k in range(1, TOP_K):
        y = y + y_ref[0, :, k * d:(k + 1) * d]
    o_ref[0] = _rms(x1_ref[0] + gate_ref[0] * y, g_ref[...])


def _final(x1, y, gate, g_final):
    b, t, d = x1.shape
    tm = min(256, t)
    tmod = gate.shape[1]
    if tmod == 1:
        mod_spec = pl.BlockSpec((1, 1, d), lambda i, j: (i, 0, 0))
    else:
        mod_spec = pl.BlockSpec((1, tm, d), lambda i, j: (i, j, 0))
    row = pl.BlockSpec((1, tm, d), lambda i, j: (i, j, 0))
    return pl.pallas_call(
        _final_kernel,
        grid=(b, t // tm),
        in_specs=[row, pl.BlockSpec((1, tm, TOP_K * d), lambda i, j: (i, j, 0)), mod_spec,
                  pl.BlockSpec((1, d), lambda i, j: (0, 0))],
        out_specs=row,
        out_shape=jax.ShapeDtypeStruct((b, t, d), F32),
        compiler_params=_cparams(("arbitrary", "arbitrary")),
        name="final",
    )(x1, y, gate, g_final.reshape(1, d))


PAGES_PER_STEP = 8


def _pool_feature_major(cache_l):
    n_pool, page_rows = cache_l.shape[:2]
    return jnp.transpose(cache_l, (0, 2, 3, 4, 1)).reshape(n_pool, -1, page_rows)


def _page_specs(pg, n_steps, n_feat, page_rows, reverse):
    def spec(i):
        def index(b, s, pt):
            grp = (n_steps - 1 - s) if reverse else s
            return (pt[b, grp * pg + i], 0, 0)
        return pl.BlockSpec((1, n_feat, page_rows), index)
    return [spec(i) for i in range(pg)]


def _sb_sample_kernel(pt_ref, q_ref, kn_ref, vn_ref, *rest, pg, ts):
    del pt_ref
    pages = rest[:pg]
    o_ref, r_ref, acc_ref = rest[pg:]
    s = pl.program_id(1)
    n_rows = q_ref.shape[1]
    seg = 2 * LANES
    q = q_ref[0]
    tri = (lax.broadcasted_iota(jnp.int32, (seg, seg), 0) > lax.broadcasted_iota(jnp.int32, (seg, seg), 1)).astype(BF16)
    q_tok = lax.broadcasted_iota(jnp.int32, (n_rows, 1), 0) % ts

    def tile(k_t, v_t, r, acc, mask):
        nseg = k_t.shape[1] // seg
        z = _dot(q, k_t)
        neg_log_keep = jnp.maximum(z, 0.0) + jnp.log(1.0 + jnp.exp(-jnp.abs(z)))
        if mask is not None:
            neg_log_keep = jnp.where(mask, neg_log_keep, 0.0)
        segs = [neg_log_keep[:, g * seg:(g + 1) * seg] for g in range(nseg)]
        stacked = segs[0] if nseg == 1 else jnp.concatenate(segs, axis=0)
        hi, lo = _split2(stacked)
        later = _dot(hi, tri) + _dot(lo, tri)
        offs = [None] * nseg
        off = r
        for g in reversed(range(nseg)):
            offs[g] = off
            off = off + jnp.sum(segs[g], axis=-1, keepdims=True)
        parts = [later[g * n_rows:(g + 1) * n_rows] + offs[g] for g in range(nseg)]
        cs = parts[0] if nseg == 1 else jnp.concatenate(parts, axis=1)
        a = jnp.exp(z - neg_log_keep - cs)
        if mask is not None:
            a = jnp.where(mask, a, 0.0)
        return off, acc + _dot_nt(a.astype(BF16), v_t)

    @pl.when(s == 0)
    def _():
        key = lax.broadcasted_iota(jnp.int32, (1, seg), 1)
        r, acc = tile(kn_ref[0], vn_ref[0], jnp.zeros((n_rows, 1), F32), jnp.zeros((n_rows, SB_WIDTH), F32),
                      key < q_tok)
        r_ref[...] = jnp.broadcast_to(r, r_ref.shape)
        acc_ref[...] = acc

    k_t = jnp.concatenate([p[0, 0:SB_WIDTH, :] for p in pages], axis=1).astype(BF16)
    v_t = jnp.concatenate([p[0, SB_WIDTH:2 * SB_WIDTH, :] for p in pages], axis=1).astype(BF16)
    r, acc = tile(k_t, v_t, r_ref[:, 0:1], acc_ref[...], None)
    r_ref[...] = jnp.broadcast_to(r, r_ref.shape)
    acc_ref[...] = acc

    @pl.when(s == pl.num_programs(1) - 1)
    def _():
        lane_head = lax.broadcasted_iota(jnp.int32, (n_rows, SB_WIDTH), 1) // HEAD_DIM
        row_head = lax.broadcasted_iota(jnp.int32, (n_rows, SB_WIDTH), 0) // ts
        own = jnp.where(lane_head == row_head, acc, 0.0)
        out = own[0:ts]
        for h in range(1, N_HEADS_SB):
            out = out + own[h * ts:(h + 1) * ts]
        o_ref[0] = out


def _sb_sample(sq_b, sbkv_new_b, pool, page_table):
    bs, ts, _ = sq_b.shape
    n_pages = page_table.shape[1]
    page_rows = pool.shape[2]
    pg = min(PAGES_PER_STEP, n_pages)
    n_steps = n_pages // pg
    n_rows = N_HEADS_SB * ts
    seg = 2 * LANES
    eye = jnp.eye(N_HEADS_SB, dtype=sq_b.dtype)
    q_rows = jnp.einsum('bqhd,hg->bhqgd', sq_b.reshape(bs, ts, N_HEADS_SB, HEAD_DIM), eye).reshape(bs, n_rows, SB_WIDTH)
    new = jnp.pad(sbkv_new_b, ((0, 0), (0, seg - ts), (0, 0))).transpose(0, 2, 1)
    k_new, v_new = new[:, :SB_WIDTH], new[:, SB_WIDTH:]
    per_b = lambda rows, w: pl.BlockSpec((1, rows, w), lambda b, s, pt: (b, 0, 0))
    grid_spec = pltpu.PrefetchScalarGridSpec(
        num_scalar_prefetch=1,
        grid=(bs, n_steps),
        in_specs=[per_b(n_rows, SB_WIDTH), per_b(SB_WIDTH, seg), per_b(SB_WIDTH, seg)]
        + _page_specs(pg, n_steps, 2 * SB_WIDTH, page_rows, True),
        out_specs=per_b(ts, SB_WIDTH),
        scratch_shapes=[pltpu.VMEM((n_rows, LANES), F32), pltpu.VMEM((n_rows, SB_WIDTH), F32)],
    )
    return pl.pallas_call(
        functools.partial(_sb_sample_kernel, pg=pg, ts=ts),
        grid_spec=grid_spec,
        out_shape=jax.ShapeDtypeStruct((bs, ts, SB_WIDTH), F32),
        compiler_params=_cparams(("arbitrary", "arbitrary")),
        name="sb_sample",
    )(page_table, q_rows, k_new, v_new, *([pool] * pg))


def _nsa_stage_kernel(pt_ref, new_ref, pek_ref, w1k_ref, w2k_ref, pev_ref, w1v_ref, w2v_ref, cos_ref, sin_ref, *rest,
                      pg, n_steps):
    del pt_ref
    pages = rest[:pg]
    kc_ref, vc_ref, lkv_ref, krows_ref, vrows_ref, bot_ref = rest[pg:]
    s = pl.program_id(1)
    n_cp = krows_ref.shape[0] // CMP_STRIDE
    page_rows = pages[0].shape[2]
    n_past_rows = n_steps * pg * page_rows
    cw = 2 * KV_WIDTH

    @pl.when(s == 0)
    def _():
        tail = n_cp * CMP_STRIDE - n_past_rows - CMP_STRIDE
        for ref, lo in ((krows_ref, 0), (vrows_ref, KV_WIDTH)):
            ref[pl.ds(n_past_rows, CMP_STRIDE), :] = new_ref[0, :, lo:lo + KV_WIDTH]
            ref[pl.ds(n_past_rows + CMP_STRIDE, tail), :] = jnp.zeros((tail, KV_WIDTH), F32)

    for i, p in enumerate(pages):
        cmp_rows = p[0, 0:cw, :].T
        base = pl.multiple_of((s * pg + i) * page_rows, page_rows)
        krows_ref[pl.ds(base, page_rows), :] = cmp_rows[:, 0:KV_WIDTH]
        vrows_ref[pl.ds(base, page_rows), :] = cmp_rows[:, KV_WIDTH:cw]
        lkv_ref[0, 0, :, i * page_rows:(i + 1) * page_rows] = p[0, cw:2 * cw, :].astype(BF16)

    @pl.when(s == n_steps - 1)
    def _():
        def store_k(g, y):
            kc_ref[0, g] = y

        def store_v(g, y):
            vc_ref[0, g] = y

        _compress_mlp(lambda r: krows_ref[pl.ds(r, n_cp, stride=CMP_STRIDE), :], n_cp, pek_ref, w1k_ref, w2k_ref,
                      bot_ref, (cos_ref, sin_ref), store_k)
        _compress_mlp(lambda r: vrows_ref[pl.ds(r, n_cp, stride=CMP_STRIDE), :], n_cp, pev_ref, w1v_ref, w2v_ref,
                      bot_ref, None, store_v)


def _nsa_stage(pool, page_table, new_cmp, cmp_k, cmp_v):
    bs, n_pages = page_table.shape
    page_rows = pool.shape[2]
    pg = min(PAGES_PER_STEP, n_pages)
    n_steps = n_pages // pg
    cpp = page_rows // CMP_STRIDE
    n_past = n_pages * cpp
    n_cp = n_past + 8
    pe_k, w1k, w2k = _prep_cmp_weights(*cmp_k)
    pe_v, w1v, w2v = _prep_cmp_weights(*cmp_v)
    hid2 = w1k.shape[2]
    cos_t, sin_t = _rope_tables(jnp.arange(n_cp) * CMP_STRIDE + CMP_BLOCK - 1)
    cw = 2 * KV_WIDTH
    const = lambda shape: pl.BlockSpec(shape, lambda b, s, pt: (0,) * len(shape))
    w_specs = [const((CMP_BLOCK, LANES)), const((CMP_BLOCK // 2, 2 * LANES, hid2)), const((hid2, 2 * LANES))]
    out_c = pl.BlockSpec((1, NSA_KV_HEADS, n_cp, LANES), lambda b, s, pt: (b, 0, 0, 0))
    grid_spec = pltpu.PrefetchScalarGridSpec(
        num_scalar_prefetch=1,
        grid=(bs, n_steps),
        in_specs=[pl.BlockSpec((1, CMP_STRIDE, cw), lambda b, s, pt: (b, 0, 0))] + w_specs + w_specs
        + [const((n_cp, LANES)), const((n_cp, LANES))] + _page_specs(pg, n_steps, 2 * cw, page_rows, False),
        out_specs=[out_c, out_c, pl.BlockSpec((1, 1, cw, pg * page_rows), lambda b, s, pt: (b, s, 0, 0))],
        scratch_shapes=[pltpu.VMEM((n_cp * CMP_STRIDE, KV_WIDTH), F32), pltpu.VMEM((n_cp * CMP_STRIDE, KV_WIDTH), F32),
                        pltpu.VMEM((n_cp + 8, hid2), F32)],
    )
    c_shape = jax.ShapeDtypeStruct((bs, NSA_KV_HEADS, n_cp, LANES), BF16)
    return pl.pallas_call(
        functools.partial(_nsa_stage_kernel, pg=pg, n_steps=n_steps),
        grid_spec=grid_spec,
        out_shape=[c_shape, c_shape, jax.ShapeDtypeStruct((bs, n_steps, cw, pg * page_rows), BF16)],
        compiler_params=_cparams(("arbitrary", "arbitrary")),
        name="nsa_stage",
    )(page_table, new_cmp, pe_k, w1k, w2k, pe_v, w1v, w2v, cos_t, sin_t, *([pool] * pg))


def _nsa_sample_kernel(q_ref, gate_ref, kc_ref, vc_ref, ovl_ref, lkv_ref, lkvn_ref, win_ref, winn_ref, o_ref, *,
                       ts, past_len, n_cmp, n_slc, n_top, tk):
    q = q_ref[0]
    n_rows = q.shape[0]
    gr = n_rows // NSA_KV_HEADS
    n_cp = kc_ref.shape[2]
    n_sp = ovl_ref.shape[1]
    q_tok = lax.broadcasted_iota(jnp.int32, (n_rows, 1), 0) % ts
    q_pos = past_len + q_tok

    c_idx = lax.broadcasted_iota(jnp.int32, (1, n_cp), 1)
    c_mask = (c_idx * CMP_STRIDE + (CMP_BLOCK - 1) <= q_pos) & (c_idx < n_cmp)
    s = jnp.concatenate([_dot_nt(q[g * gr:(g + 1) * gr], kc_ref[0, g]) for g in range(NSA_KV_HEADS)], axis=0)
    s = jnp.where(c_mask, s, NEG)
    m = jnp.max(s, axis=-1, keepdims=True)
    e = jnp.where(c_mask, jnp.exp(s - m), 0.0)
    p_cmp = e / jnp.maximum(jnp.sum(e, axis=-1, keepdims=True), 1.0)
    p_b = p_cmp.astype(BF16)
    o_cmp = jnp.concatenate([_dot(p_b[g * gr:(g + 1) * gr], vc_ref[0, g]) for g in range(NSA_KV_HEADS)], axis=0)

    sums = []
    for g in range(NSA_KV_HEADS):
        acc = p_cmp[g * gr:g * gr + ts]
        for h in range(1, NSA_GROUP):
            acc = acc + p_cmp[g * gr + h * ts:g * gr + (h + 1) * ts]
        sums.append(acc)
    p_sum = jnp.concatenate(sums, axis=0)
    n_imp = p_sum.shape[0]
    p_hi, p_lo = _split2(p_sum)
    p_lo2 = (p_sum - p_hi.astype(F32) - p_lo.astype(F32)).astype(BF16)
    ovl = ovl_ref[...]
    imp = _dot(p_hi, ovl) + _dot(p_lo, ovl) + _dot(p_lo2, ovl)
    blk = lax.broadcasted_iota(jnp.int32, (n_imp, n_sp), 1)
    qp = past_len + lax.broadcasted_iota(jnp.int32, (n_imp, 1), 0) % ts
    cur = qp // SLC_BLOCK
    forced = (blk == 0) | (blk == cur) | (blk == cur - 1)
    imp = jnp.where(blk * SLC_BLOCK <= qp, imp + jnp.where(forced, FORCE_BONUS, 0.0), NEG)
    imp = jnp.where(blk < n_slc, imp, -jnp.inf)

    def pick(_, carry):
        imp_c, unsel = carry
        best = jnp.max(imp_c, axis=-1, keepdims=True)
        first = jnp.min(jnp.where(imp_c == best, blk, n_sp), axis=-1, keepdims=True)
        hit = blk == first
        return jnp.where(hit, -jnp.inf, imp_c), jnp.where(hit, 0.0, unsel)

    _, unsel = lax.fori_loop(0, n_top, pick, (imp, jnp.ones((n_imp, n_sp), F32)))
    unsel = unsel.astype(BF16)
    unsel_rows = jnp.concatenate([unsel[g * ts:(g + 1) * ts] for g in range(NSA_KV_HEADS) for _ in range(NSA_GROUP)],
                                 axis=0)
    q_aug = jnp.concatenate([q, unsel_rows], axis=1)

    def update(carry, sc, v_t):
        m_run, l_run, acc = carry
        m_new = jnp.maximum(m_run, jnp.max(sc, axis=-1, keepdims=True))
        alpha = jnp.exp(m_run - m_new)
        pr = jnp.exp(sc - m_new)
        return (m_new, alpha * l_run + jnp.sum(pr, axis=-1, keepdims=True),
                alpha * acc + _dot_nt(pr.astype(BF16), v_t))

    def bias_rows(first_pos, n_keys):
        key_blk = (first_pos + lax.broadcasted_iota(jnp.int32, (n_sp, n_keys), 1)) // SLC_BLOCK
        return jnp.where(lax.broadcasted_iota(jnp.int32, (n_sp, n_keys), 0) == key_blk, MASK_BIAS, 0.0).astype(BF16)

    def slc_tile(j, carry):
        k_t = lkv_ref[0, j, 0:KV_WIDTH, :]
        v_t = lkv_ref[0, j, KV_WIDTH:2 * KV_WIDTH, :]
        sc = _dot(q_aug, jnp.concatenate([k_t, bias_rows(j * tk, tk)], axis=0))
        return update(carry, sc, v_t)

    init = (jnp.full((n_rows, 1), NEG, F32), jnp.zeros((n_rows, 1), F32), jnp.zeros((n_rows, KV_WIDTH), F32))
    carry = lax.fori_loop(0, past_len // tk, slc_tile, init)
    n_new = lkvn_ref.shape[2]
    new_idx = lax.broadcasted_iota(jnp.int32, (1, n_new), 1)
    kn_t = lkvn_ref[0, 0:KV_WIDTH, :]
    vn_t = lkvn_ref[0, KV_WIDTH:2 * KV_WIDTH, :]
    sc = _dot(q_aug, jnp.concatenate([kn_t, bias_rows(past_len, n_new)], axis=0))
    sc = jnp.where(new_idx <= q_tok, sc, MASK_BIAS)
    _, l_fin, acc = update(carry, sc, vn_t)
    o_slc = acc / jnp.maximum(l_fin, 1.0)

    wl = win_ref.shape[1]
    kw = win_ref[0, :, 0:KV_WIDTH].astype(BF16)
    vw = win_ref[0, :, KV_WIDTH:2 * KV_WIDTH].astype(BF16)
    kwn = winn_ref[0, :, 0:KV_WIDTH]
    vwn = winn_ref[0, :, KV_WIDTH:2 * KV_WIDTH]
    w_pos = past_len - wl + lax.broadcasted_iota(jnp.int32, (1, wl), 1)
    dist = q_pos - w_pos
    mask_w = (w_pos >= 0) & (dist >= 0) & (dist < WINDOW)
    dist_n = q_tok - new_idx
    mask_n = (dist_n >= 0) & (dist_n < WINDOW)
    mask = jnp.concatenate([mask_w, mask_n], axis=1)
    sw = jnp.where(mask, jnp.concatenate([_dot_nt(q, kw), _dot_nt(q, kwn)], axis=1), NEG)
    mw = jnp.max(sw, axis=-1, keepdims=True)
    ew = jnp.where(mask, jnp.exp(sw - mw), 0.0)
    pw = (ew / jnp.maximum(jnp.sum(ew, axis=-1, keepdims=True), 1.0)).astype(BF16)
    o_win = _dot(pw[:, :wl], vw) + _dot(pw[:, wl:], vwn)

    gates = gate_ref[0]
    o_ref[0] = gates[:, 0:1] * o_cmp + gates[:, 1:2] * o_slc + gates[:, 2:3] * o_win


def _nsa_sample(nq_b, gates, kc, vc, lkv, nsakv_new, win_state, winkv_new, past_len):
    bs, ts, _ = nq_b.shape
    n_cp = kc.shape[2]
    n_cmp = (past_len + ts + CMP_STRIDE - 1) // CMP_STRIDE - 1
    n_slc = -(-(past_len + ts) // SLC_BLOCK)
    n_sp = -(-n_slc // LANES) * LANES
    n_top = min(SLC_TOP_N, n_slc)
    tk = lkv.shape[3]
    n_rows = N_HEADS_NSA * ts
    q5 = nq_b.reshape(bs, ts, NSA_KV_HEADS, NSA_GROUP, HEAD_DIM).transpose(0, 2, 3, 1, 4)
    zq = jnp.zeros_like(q5[:, 0])
    q_rows = jnp.stack([jnp.concatenate([q5[:, 0], zq], axis=-1), jnp.concatenate([zq, q5[:, 1]], axis=-1)], axis=1)
    q_rows = q_rows.reshape(bs, n_rows, KV_WIDTH)
    g_rows = gates.reshape(bs, ts, NSA_KV_HEADS, NSA_GROUP, 3).transpose(0, 2, 3, 1, 4).reshape(bs, n_rows, 3)
    g_rows = jnp.pad(g_rows, ((0, 0), (0, 0), (0, LANES - 3)))
    pad_new = lambda x: jnp.pad(x, ((0, 0), (0, LANES - ts), (0, 0))).astype(BF16)
    lkv_new = pad_new(nsakv_new[:, :, 2 * KV_WIDTH:]).transpose(0, 2, 1)
    win_new = pad_new(winkv_new)
    cs = jnp.arange(n_cp) * CMP_STRIDE
    ss = jnp.arange(n_sp) * SLC_BLOCK
    ovl = ((cs[:, None] < ss[None, :] + SLC_BLOCK) & (cs[:, None] + CMP_BLOCK > ss[None, :])
           & (jnp.arange(n_cp)[:, None] < n_cmp) & (jnp.arange(n_sp)[None, :] < n_slc)).astype(BF16)
    wl = win_state.shape[1]
    per_b = lambda shape: pl.BlockSpec((1,) + shape, lambda b: (b,) + (0,) * len(shape))
    out = pl.pallas_call(
        functools.partial(_nsa_sample_kernel, ts=ts, past_len=past_len, n_cmp=n_cmp, n_slc=n_slc, n_top=n_top, tk=tk),
        grid=(bs,),
        in_specs=[per_b((n_rows, KV_WIDTH)), per_b((n_rows, LANES)),
                  per_b((NSA_KV_HEADS, n_cp, LANES)), per_b((NSA_KV_HEADS, n_cp, LANES)),
                  pl.BlockSpec((n_cp, n_sp), lambda b: (0, 0)),
                  per_b(lkv.shape[1:]), per_b((2 * KV_WIDTH, LANES)),
                  per_b((wl, 2 * KV_WIDTH)), per_b((LANES, 2 * KV_WIDTH))],
        out_specs=per_b((n_rows, KV_WIDTH)),
        out_shape=jax.ShapeDtypeStruct((bs, n_rows, KV_WIDTH), F32),
        compiler_params=_cparams(("arbitrary",)),
        name="nsa_sample",
    )(q_rows, g_rows, kc, vc, ovl, lkv, lkv_new, win_state, win_new)
    o5 = out.reshape(bs, NSA_KV_HEADS, NSA_GROUP, ts, NSA_KV_HEADS, HEAD_DIM)
    o = jnp.stack([o5[:, g, :, :, g] for g in range(NSA_KV_HEADS)], axis=1)
    return o.transpose(0, 3, 1, 2, 4).reshape(bs, ts, NSA_WIDTH)


def _rope(x, pos):
    half = HEAD_DIM // 2
    inv_freq = jnp.power(ROPE_THETA, -jnp.arange(half, dtype=F32) / half)
    ang = pos.astype(F32)[:, None] * inv_freq[None, :]
    cos = jnp.cos(ang)[None, :, None, :]
    sin = jnp.sin(ang)[None, :, None, :]
    x1, x2 = x[..., :half], x[..., half:]
    return jnp.concatenate([x1 * cos - x2 * sin, x2 * cos + x1 * sin], axis=-1)


def _masked_softmax(s, mask):
    s = jnp.where(mask, s, NEG)
    m = jnp.max(s, axis=-1, keepdims=True)
    e = jnp.where(mask, jnp.exp(s - m), 0.0)
    return e / jnp.maximum(jnp.sum(e, axis=-1, keepdims=True), 1.0)


def _sb_attend(q, q_pos, segments):
    scale = HEAD_DIM ** -0.5
    z = jnp.concatenate([jnp.einsum('bqhd,bkhd->bhqk', q, k).astype(F32) for k, _, _ in segments], axis=-1) * scale
    k_pos = jnp.concatenate([p for _, _, p in segments])
    mask = k_pos[None, :] < q_pos[:, None]
    log_keep = jnp.where(mask, jax.nn.log_sigmoid(-z), 0.0)
    log_stick = lax.cumsum(log_keep, axis=3, reverse=True) - log_keep
    a = jnp.where(mask, jnp.exp(jax.nn.log_sigmoid(z) + log_stick), 0.0)
    outs = []
    off = 0
    for k, v, _ in segments:
        n = k.shape[1]
        outs.append(jnp.einsum('bhqk,bkhd->bqhd', a[..., off:off + n], v))
        off += n
    return sum(outs[1:], outs[0])


def _compress_jax(rows, pe, w1, w2):
    b, t, g, d = rows.shape
    t_pad = -(-t // CMP_STRIDE) * CMP_STRIDE
    rows = jnp.pad(rows, ((0, 0), (0, t_pad - t), (0, 0), (0, 0)))
    n_chunk = t_pad // CMP_STRIDE
    ratio = CMP_BLOCK // CMP_STRIDE
    n_cmp = n_chunk - ratio + 1
    ch = rows.reshape(b, n_chunk, CMP_STRIDE, g, d)
    blocks = jnp.concatenate([ch[:, r:r + n_cmp] for r in range(ratio)], axis=2)
    blocks = blocks + pe[:, None, :]
    flat = blocks.transpose(0, 1, 3, 2, 4).reshape(b, n_cmp, g, CMP_BLOCK * d)
    out = jax.nn.silu(flat @ w1) @ w2
    end = jnp.arange(n_cmp) * CMP_STRIDE + CMP_BLOCK - 1
    return out, end


def _cmp_to_slc(n_cmp, n_slc):
    cs = jnp.arange(n_cmp) * CMP_STRIDE
    ss = jnp.arange(n_slc) * SLC_BLOCK
    return ((cs[:, None] < ss[None, :] + SLC_BLOCK) & (cs[:, None] + CMP_BLOCK > ss[None, :])).astype(F32)


def _nsa_attend(q, q_pos, kc, vc, c_end, ks, vs, kw, vw, w_pos, gates):
    b, tq = q.shape[:2]
    scale = HEAD_DIM ** -0.5
    qg = q.reshape(b, tq, NSA_KV_HEADS, NSA_GROUP, HEAD_DIM)
    s = jnp.einsum('bqgrd,bngd->bgrqn', qg, kc).astype(F32) * scale
    p_cmp = _masked_softmax(s, c_end[None, :] <= q_pos[:, None])
    o_cmp = jnp.einsum('bgrqn,bngd->bqgrd', p_cmp, vc)
    tk = ks.shape[1]
    n_slc = -(-tk // SLC_BLOCK)
    n_top = min(SLC_TOP_N, n_slc)
    imp = jnp.einsum('bgrqn,ns->bgqs', p_cmp, _cmp_to_slc(kc.shape[1], n_slc), precision=lax.Precision.HIGHEST)
    blk = jnp.arange(n_slc)[None, :]
    cur = (q_pos // SLC_BLOCK)[:, None]
    forced = (blk == 0) | (blk == cur) | (blk == cur - 1)
    imp = jnp.where(blk * SLC_BLOCK <= q_pos[:, None], imp + jnp.where(forced, FORCE_BONUS, 0.0), NEG)
    _, idx = lax.top_k(imp, n_top)
    pad = ((0, 0), (0, n_slc * SLC_BLOCK - tk), (0, 0), (0, 0))

    def to_blocks(t):
        return jnp.pad(t, pad).reshape(b, n_slc, SLC_BLOCK, NSA_KV_HEADS, HEAD_DIM).transpose(0, 3, 1, 2, 4)

    take = jax.vmap(jax.vmap(lambda blocks, ids: blocks[ids]))
    n_keys = n_top * SLC_BLOCK
    kg = take(to_blocks(ks), idx).reshape(b, NSA_KV_HEADS, tq, n_keys, HEAD_DIM)
    vg = take(to_blocks(vs), idx).reshape(b, NSA_KV_HEADS, tq, n_keys, HEAD_DIM)
    k_pos = (idx[..., None] * SLC_BLOCK + jnp.arange(SLC_BLOCK)).reshape(b, NSA_KV_HEADS, 1, tq, n_keys)
    s = jnp.einsum('bqgrd,bgqmd->bgrqm', qg, kg).astype(F32) * scale
    p = _masked_softmax(s, k_pos <= q_pos[:, None])
    o_slc = jnp.einsum('bgrqm,bgqmd->bqgrd', p, vg)
    s = jnp.einsum('bqgrd,bwgd->bgrqw', qg, kw).astype(F32) * scale
    dist = q_pos[:, None] - w_pos[None, :]
    p = _masked_softmax(s, (w_pos[None, :] >= 0) & (dist >= 0) & (dist < WINDOW))
    o_win = jnp.einsum('bgrqw,bwgd->bqgrd', p, vw)
    gt = gates.reshape(b, tq, NSA_KV_HEADS, NSA_GROUP, 3)
    o = gt[..., 0:1] * o_cmp + gt[..., 1:2] * o_slc + gt[..., 2:3] * o_win
    return o.reshape(b, tq, NSA_WIDTH)


def _gather_pages(pool, layer, page_table, slot):
    rows = pool[layer, page_table, :, slot]
    return rows.reshape(rows.shape[0], rows.shape[1] * rows.shape[2], rows.shape[3], rows.shape[4])


def _mix_out(x, o_sb, o_nsa, mods, lw, g_final):
    (gate_attn, shift_ffn, scale_ffn, gate_ffn) = mods
    (g_sb_out, g_nsa_out, w_out_b, g_ffn, w_router, b_router, w1_b, b_e1, w2_b, b_e2) = lw
    b, t, d = x.shape
    x1, h_b, logits = _outproj(x, o_sb, o_nsa, gate_attn, shift_ffn, scale_ffn, g_sb_out, g_nsa_out, g_ffn,
                               w_out_b, w_router, b_router)
    y = _moe(h_b.reshape(b * t, d), logits.reshape(b * t, LANES)[:, :N_EXPERTS], w1_b, b_e1, w2_b, b_e2)
    return _final(x1, y.reshape(b, t, TOP_K * d), gate_ffn, g_final)


def kernel(x_prompt, x_sample, c_prompt, c_sample, cache_sb_kv, cache_nsa_kv, state_win_kv, page_table, w_ada, b_ada, g_attn, w_in, g_sb_out, g_nsa_out, w_out, pe_k, w_ck1, w_ck2, pe_v, w_cv1, w_cv2, g_ffn, w_router, b_router, w_e1, b_e1, w_e2, b_e2, g_final):
    depth = w_ada.shape[0]
    assert depth == 1
    l = 0
    bp, t, d = x_prompt.shape
    bs, ts, _ = x_sample.shape

    w_all = _prep_w_in(w_in[l])
    w_out_b = w_out[l].astype(BF16)
    w1_b = w_e1[l].astype(BF16)
    w2_b = w_e2[l].astype(BF16)
    out_lw = (g_sb_out[l], g_nsa_out[l], w_out_b, g_ffn[l], w_router[l], b_router[l], w1_b, b_e1[l], w2_b, b_e2[l])

    n_c = bp + bs
    n_c_pad = -(-n_c // 8) * 8
    c_all = jnp.pad(jnp.concatenate([c_prompt, c_sample]), ((0, n_c_pad - n_c), (0, 0)))
    mod_all = _adaln(c_all, w_ada[l].astype(BF16), b_ada[l]).reshape(n_c_pad, 6, d)
    mp = [mod_all[:bp, i][:, None, :] for i in range(6)]
    ms = [jnp.repeat(mod_all[bp:n_c, i], ts, axis=0)[None] for i in range(6)]

    pos = jnp.arange(t)
    sq_b, sbkv, sbkv_b, nq_b, nsakv, winkv, dup_b, gates = _inproj(x_prompt, mp[0], mp[1], g_attn[l], w_all, pos)
    o_sb = _sb_prompt(sq_b, sbkv_b)
    kc = _compress(nsakv, 0, pe_k[l], w_ck1[l], w_ck2[l], True)
    vc = _compress(nsakv, 1, pe_v[l], w_cv1[l], w_cv2[l], False)
    o_nsa = _nsa_prompt(nq_b, gates, kc, vc, dup_b)
    y_prompt = _mix_out(x_prompt, o_sb, o_nsa, (mp[2], mp[3], mp[4], mp[5]), out_lw, g_final)
    keep = min(WINDOW, t)
    sb_kv_prompt = sbkv.reshape(1, bp, t, 2, N_HEADS_SB, HEAD_DIM)
    nsa_kv_prompt = nsakv.reshape(1, bp, t, 4, NSA_KV_HEADS, HEAD_DIM)
    win_kv_prompt = winkv[:, t - keep:].reshape(1, bp, keep, 2, NSA_KV_HEADS, HEAD_DIM)

    past_len = page_table.shape[1] * cache_sb_kv.shape[2]
    q_pos = past_len + jnp.arange(ts)
    pos_rows = jnp.tile(q_pos, bs)
    xs_flat = x_sample.reshape(1, bs * ts, d)
    s_sq_b, s_sbkv, s_sbkv_b, s_nq_b, s_nsakv, s_winkv, _, s_gates = _inproj(xs_flat, ms[0], ms[1], g_attn[l], w_all,
                                                                              pos_rows)
    per_seq = lambda a: a.reshape(bs, ts, a.shape[-1])
    sb_pool = _pool_feature_major(cache_sb_kv[l])
    nsa_pool = _pool_feature_major(cache_nsa_kv[l])
    o_sb_s = _sb_sample(per_seq(s_sq_b), per_seq(s_sbkv_b), sb_pool, page_table)
    nsakv_s = per_seq(s_nsakv)
    winkv_s = per_seq(s_winkv)
    new_cmp = jnp.pad(nsakv_s[:, :, :2 * KV_WIDTH], ((0, 0), (0, CMP_STRIDE - ts), (0, 0)))
    kc_s, vc_s, lkv = _nsa_stage(nsa_pool, page_table, new_cmp, (pe_k[l], w_ck1[l], w_ck2[l]),
                                 (pe_v[l], w_cv1[l], w_cv2[l]))
    win_state = state_win_kv[l].reshape(bs, -1, 2 * KV_WIDTH)
    o_nsa_s = _nsa_sample(per_seq(s_nq_b), per_seq(s_gates)[:, :, :N_GATES], kc_s, vc_s, lkv, nsakv_s, win_state,
                          winkv_s, past_len)
    y_sample = _mix_out(xs_flat, o_sb_s.reshape(1, bs * ts, SB_WIDTH), o_nsa_s.reshape(1, bs * ts, NSA_WIDTH),
                        (ms[2], ms[3], ms[4], ms[5]), out_lw, g_final).reshape(bs, ts, d)
    sb_kv_sample = per_seq(s_sbkv).reshape(1, bs, ts, 2, N_HEADS_SB, HEAD_DIM)
    nsa_kv_sample = nsakv_s.reshape(1, bs, ts, 4, NSA_KV_HEADS, HEAD_DIM)
    win_kv_sample = jnp.concatenate([win_state, winkv_s], axis=1)[:, ts:].reshape(1, bs, -1, 2, NSA_KV_HEADS, HEAD_DIM)
    return (y_prompt, y_sample, sb_kv_prompt, nsa_kv_prompt, win_kv_prompt, sb_kv_sample, nsa_kv_sample, win_kv_sample)
```

```python
import functools

import jax
import jax.numpy as jnp
from jax import lax
from jax.experimental import pallas as pl
from jax.experimental.pallas import tpu as pltpu

HEAD_DIM = 64
N_HEADS_SB = 8
N_HEADS_NSA = 8
NSA_KV_HEADS = 2
NSA_GROUP = N_HEADS_NSA // NSA_KV_HEADS
SB_WIDTH = N_HEADS_SB * HEAD_DIM
NSA_WIDTH = N_HEADS_NSA * HEAD_DIM
KV_WIDTH = NSA_KV_HEADS * HEAD_DIM
N_GATES = 3 * N_HEADS_NSA
CMP_BLOCK = 32
CMP_STRIDE = 16
SLC_BLOCK = 64
SLC_TOP_N = 16
WINDOW = 512
N_EXPERTS = 32
TOP_K = 4
SWIGLU_LIMIT = 7.0
SWIGLU_ALPHA = 1.702
MOE_ROW_BLOCK = 256
ROPE_THETA = 10000.0
EPS = 1e-6
NEG = -1e30
FORCE_BONUS = 1e3
MASK_BIAS = -(2.0 ** 30)
LOG2_E = 1.4426950408889634

LANES = 128
VMEM_LIMIT = 56 * 1024 * 1024

BF16 = jnp.bfloat16
F32 = jnp.float32


def _cparams(sem):
    return pltpu.CompilerParams(dimension_semantics=sem, vmem_limit_bytes=VMEM_LIMIT)


def _dot(a, b):
    return jnp.dot(a, b, preferred_element_type=F32)


def _dot_nt(a, b):
    return lax.dot_general(a, b, (((1,), (1,)), ((), ())), preferred_element_type=F32)


def _split2(x):
    hi = x.astype(BF16)
    lo = (x - hi.astype(F32)).astype(BF16)
    return hi, lo


def _rms(x, g):
    return (x * lax.rsqrt(jnp.mean(x * x, axis=-1, keepdims=True) + EPS)) * g


def _rope_slab(x, cos, sin_signed):
    lane = lax.broadcasted_iota(jnp.int32, x.shape, 1)
    first = (lane % HEAD_DIM) < (HEAD_DIM // 2)
    partner = jnp.where(first, pltpu.roll(x, LANES - HEAD_DIM // 2, 1), pltpu.roll(x, HEAD_DIM // 2, 1))
    return x * cos + partner * sin_signed


def _adaln_kernel(c_ref, w_ref, b_ref, o_ref):
    c = c_ref[...]
    s = (c * jax.nn.sigmoid(c)).astype(BF16)
    o_ref[...] = _dot(s, w_ref[...]) + b_ref[...]


def _adaln(c, w_ada_b, b_ada):
    r, d = c.shape
    n = w_ada_b.shape[1]
    tn = d
    return pl.pallas_call(
        _adaln_kernel,
        grid=(n // tn,),
        in_specs=[pl.BlockSpec((r, d), lambda j: (0, 0)),
                  pl.BlockSpec((d, tn), lambda j: (0, j)),
                  pl.BlockSpec((1, tn), lambda j: (0, j))],
        out_specs=pl.BlockSpec((r, tn), lambda j: (0, j)),
        out_shape=jax.ShapeDtypeStruct((r, n), F32),
        compiler_params=_cparams(("arbitrary",)),
        name="adaln",
    )(c, w_ada_b, b_ada.reshape(1, n))


_C_SQ = 0
_C_SBKV = SB_WIDTH
_C_NQ = 3 * SB_WIDTH
_C_NSAKV = _C_NQ + NSA_WIDTH
_C_WINKV = _C_NSAKV + 4 * KV_WIDTH
_C_DUP = _C_WINKV + 2 * KV_WIDTH
_C_GATE = _C_DUP + 8 * LANES
_C_END = _C_GATE + LANES


def _prep_w_in(w_in):
    main = w_in[:, :_C_DUP]
    dups = []
    for g in range(NSA_KV_HEADS):
        for slot in (2, 3, 4, 5):
            c0 = _C_NSAKV + slot * KV_WIDTH + g * HEAD_DIM
            col = w_in[:, c0:c0 + HEAD_DIM]
            dups += [col, col]
    gates = jnp.pad(w_in[:, _C_DUP:_C_DUP + N_GATES], ((0, 0), (0, LANES - N_GATES)))
    return jnp.concatenate([main] + dups + [gates], axis=1).astype(BF16)


def _inproj_kernel(x_ref, shift_ref, scale_ref, g_ref, w_ref, cos_ref, sin_ref,
                   sq_ref, sbkv_ref, sbkvb_ref, nq_ref, nsakv_ref, winkv_ref, dup_ref, gate_ref):
    x = x_ref[0]
    h = _rms(x, g_ref[...]) * (1.0 + scale_ref[0]) + shift_ref[0]
    hb = h.astype(BF16)
    cos = cos_ref[...]
    sin = sin_ref[...]
    qscale = HEAD_DIM ** -0.5

    def proj(c0, width):
        return _dot(hb, w_ref[:, c0:c0 + width])

    sq_ref[0] = (proj(_C_SQ, SB_WIDTH) * (qscale * LOG2_E)).astype(BF16)
    kv = proj(_C_SBKV, 2 * SB_WIDTH)
    sbkv_ref[0] = kv
    sbkvb_ref[0] = kv.astype(BF16)
    for p in range(NSA_WIDTH // LANES):
        y = _rope_slab(proj(_C_NQ + p * LANES, LANES), cos, sin)
        nq_ref[0, :, p * LANES:(p + 1) * LANES] = (y * qscale).astype(BF16)
    for s in range(4):
        y = proj(_C_NSAKV + s * LANES, LANES)
        if s == 2:
            y = _rope_slab(y, cos, sin)
        nsakv_ref[0, :, s * LANES:(s + 1) * LANES] = y
    for s in range(2):
        y = proj(_C_WINKV + s * LANES, LANES)
        if s == 0:
            y = _rope_slab(y, cos, sin)
        winkv_ref[0, :, s * LANES:(s + 1) * LANES] = y
    for s in range(8):
        y = proj(_C_DUP + s * LANES, LANES)
        if s % 2 == 0:
            y = _rope_slab(y, cos, sin)
        dup_ref[0, :, s * LANES:(s + 1) * LANES] = y.astype(BF16)
    gate_ref[0] = jax.nn.sigmoid(proj(_C_GATE, LANES))


def _rope_tables(pos):
    half = HEAD_DIM // 2
    inv_freq = jnp.power(ROPE_THETA, -jnp.arange(half, dtype=F32) / half)
    ang = pos.astype(F32)[:, None] * inv_freq[None, :]
    cos, sin = jnp.cos(ang), jnp.sin(ang)
    cos_t = jnp.concatenate([cos, cos, cos, cos], axis=1)
    sin_t = jnp.concatenate([-sin, sin, -sin, sin], axis=1)
    return cos_t, sin_t


def _inproj(x, shift, scale, g_attn, w_all, pos):
    b, t, d = x.shape
    tm = min(256, t)
    tmod = shift.shape[1]
    cos_t, sin_t = _rope_tables(pos)
    if tmod == 1:
        mod_spec = pl.BlockSpec((1, 1, d), lambda i, j: (i, 0, 0))
    else:
        mod_spec = pl.BlockSpec((1, tm, d), lambda i, j: (i, j, 0))
    widths = (SB_WIDTH, 2 * SB_WIDTH, 2 * SB_WIDTH, NSA_WIDTH, 4 * KV_WIDTH, 2 * KV_WIDTH, 8 * LANES, LANES)
    dtypes = (BF16, F32, BF16, BF16, F32, F32, BF16, F32)
    return pl.pallas_call(
        _inproj_kernel,
        grid=(b, t // tm),
        in_specs=[pl.BlockSpec((1, tm, d), lambda i, j: (i, j, 0)),
                  mod_spec, mod_spec,
                  pl.BlockSpec((1, d), lambda i, j: (0, 0)),
                  pl.BlockSpec((d, _C_END), lambda i, j: (0, 0)),
                  pl.BlockSpec((tm, LANES), lambda i, j: (j, 0)),
                  pl.BlockSpec((tm, LANES), lambda i, j: (j, 0))],
        out_specs=[pl.BlockSpec((1, tm, w), lambda i, j: (i, j, 0)) for w in widths],
        out_shape=[jax.ShapeDtypeStruct((b, t, w), dt) for w, dt in zip(widths, dtypes)],
        compiler_params=_cparams(("arbitrary", "arbitrary")),
        name="inproj",
    )(x, shift, scale, g_attn.reshape(1, d), w_all, cos_t, sin_t)


SB_TILE_GROUP = 4


def _neg_log2_keep(z):
    return jnp.maximum(z, 0.0) + jnp.log2(1.0 + jnp.exp2(-jnp.abs(z)))


def _sb_scores(q_h, k, tri, mask):
    z = _dot_nt(q_h, k)
    neg_log_keep = _neg_log2_keep(z)
    if mask is not None:
        neg_log_keep = jnp.where(mask, neg_log_keep, 0.0)
    hi, lo = _split2(neg_log_keep)
    later = _dot(hi, tri) + _dot(lo, tri)
    return z - neg_log_keep - later, jnp.sum(neg_log_keep, axis=-1, keepdims=True)


def _sb_kernel(q_ref, k_ref, v_ref, o_ref, *, tq):
    qi = pl.program_id(2)
    q = q_ref[0]
    lane = lax.broadcasted_iota(jnp.int32, (1, LANES), 1)
    head_lanes = [lane < HEAD_DIM, lane >= HEAD_DIM]
    zero = jnp.zeros((), BF16)
    q_heads = [jnp.where(m, q, zero) for m in head_lanes]
    row = lax.broadcasted_iota(jnp.int32, (tq, tq), 0)
    col = lax.broadcasted_iota(jnp.int32, (tq, tq), 1)
    tri = (row > col).astype(BF16)
    diag_mask = col < row

    def tiles(js, carry, mask):
        rs = list(carry[:2])
        acc = carry[2]
        parts = []
        for j in js:
            start = pl.multiple_of(j * tq, tq)
            k = k_ref[0, pl.ds(start, tq), :]
            v = v_ref[0, pl.ds(start, tq), :]
            parts.append([(_sb_scores(q_heads[hh], k, tri, mask), jnp.where(head_lanes[hh], v, zero))
                          for hh in range(2)])
        for per_head in parts:
            for hh in range(2):
                (expo, total), v_h = per_head[hh]
                a = jnp.exp2(expo - rs[hh])
                if mask is not None:
                    a = jnp.where(mask, a, 0.0)
                acc = acc + _dot(a.astype(BF16), v_h)
                rs[hh] = rs[hh] + total
        return rs[0], rs[1], acc

    init = (jnp.zeros((tq, 1), F32), jnp.zeros((tq, 1), F32), jnp.zeros((tq, LANES), F32))
    carry = tiles([qi], init, diag_mask)
    rem = qi % SB_TILE_GROUP
    carry = lax.fori_loop(0, rem, lambda s, c: tiles([qi - 1 - s], c, None), carry)
    top = qi - rem
    carry = lax.fori_loop(
        0, top // SB_TILE_GROUP,
        lambda s, c: tiles([top - 1 - SB_TILE_GROUP * s - i for i in range(SB_TILE_GROUP)], c, None), carry)
    o_ref[0] = carry[2]


def _sb_prompt(sq_b, sbkv_b):
    b, t, _ = sq_b.shape
    tq = min(256, t)
    n_pair = SB_WIDTH // LANES
    return pl.pallas_call(
        functools.partial(_sb_kernel, tq=tq),
        grid=(b, n_pair, t // tq),
        in_specs=[pl.BlockSpec((1, tq, LANES), lambda i, p, j: (i, j, p)),
                  pl.BlockSpec((1, t, LANES), lambda i, p, j: (i, 0, p)),
                  pl.BlockSpec((1, t, LANES), lambda i, p, j: (i, 0, n_pair + p))],
        out_specs=pl.BlockSpec((1, tq, LANES), lambda i, p, j: (i, j, p)),
        out_shape=jax.ShapeDtypeStruct((b, t, SB_WIDTH), F32),
        compiler_params=_cparams(("arbitrary", "arbitrary", "arbitrary")),
        name="sb_prompt",
    )(sq_b, sbkv_b, sbkv_b)


def _prep_cmp_weights(pe, w1, w2):
    hid = w1.shape[1]
    w1r = w1.reshape(CMP_BLOCK, HEAD_DIM, hid)
    z = jnp.zeros_like(w1r)
    w1_bd = jnp.concatenate([jnp.concatenate([w1r, z], axis=2), jnp.concatenate([z, w1r], axis=2)], axis=1)
    w2d = jnp.concatenate([w2, w2], axis=1)
    z2 = jnp.zeros_like(w2d)
    w2_bd = jnp.concatenate([jnp.concatenate([w2d, z2], axis=1), jnp.concatenate([z2, w2d], axis=1)], axis=0)
    pe_d = jnp.concatenate([pe, pe], axis=1)
    w1_pair = w1_bd.reshape(CMP_BLOCK // 2, 2 * LANES, 2 * hid)
    return pe_d, w1_pair.astype(BF16), w2_bd.astype(BF16)


def _compress_mlp(load_row, n, pe_ref, w1_ref, w2_ref, bot_ref, cos_sin, store):
    hid2 = w1_ref.shape[2]
    half = CMP_STRIDE // 2
    top = jnp.zeros((n, hid2), F32)
    bot = jnp.zeros((n, hid2), F32)
    for rp in range(half):
        xa = load_row(2 * rp)
        xb = load_row(2 * rp + 1)

        def lhs(off):
            r0 = off + 2 * rp
            return jnp.concatenate([xa + pe_ref[r0:r0 + 1, :], xb + pe_ref[r0 + 1:r0 + 2, :]], axis=1).astype(BF16)

        top = top + _dot(lhs(0), w1_ref[rp])
        bot = bot + _dot(lhs(CMP_STRIDE), w1_ref[half + rp])
    bot_ref[pl.ds(0, n), :] = bot
    bot_ref[pl.ds(n, 8), :] = jnp.zeros((8, hid2), F32)
    hidden = top + bot_ref[pl.ds(1, n), :]
    hidden = hidden * jax.nn.sigmoid(hidden)
    out = _dot(hidden.astype(BF16), w2_ref[...])
    for g in range(NSA_KV_HEADS):
        y = out[:, g * LANES:(g + 1) * LANES]
        if cos_sin is not None:
            y = _rope_slab(y, cos_sin[0][...], cos_sin[1][...])
        store(g, y.astype(BF16))


def _compress_kernel(x_ref, pe_ref, w1_ref, w2_ref, cos_ref, sin_ref, o_ref, bot_ref, *, rope):
    n = x_ref.shape[1] // CMP_STRIDE

    def store(g, y):
        o_ref[0, g] = y

    _compress_mlp(lambda r: x_ref[0, pl.ds(r, n, stride=CMP_STRIDE), :], n, pe_ref, w1_ref, w2_ref, bot_ref,
                  (cos_ref, sin_ref) if rope else None, store)


def _compress(rows4, slot, pe, w1, w2, rope):
    b, t, _ = rows4.shape
    n = t // CMP_STRIDE
    pe_d, w1_bd, w2_bd = _prep_cmp_weights(pe, w1, w2)
    c_end = jnp.arange(n) * CMP_STRIDE + CMP_BLOCK - 1
    cos_t, sin_t = _rope_tables(c_end)
    hid2 = w1_bd.shape[2]
    return pl.pallas_call(
        functools.partial(_compress_kernel, rope=rope),
        grid=(b,),
        in_specs=[pl.BlockSpec((1, t, LANES), lambda i: (i, 0, slot)),
                  pl.BlockSpec((CMP_BLOCK, LANES), lambda i: (0, 0)),
                  pl.BlockSpec((CMP_BLOCK // 2, 2 * LANES, hid2), lambda i: (0, 0, 0)),
                  pl.BlockSpec((hid2, 2 * LANES), lambda i: (0, 0)),
                  pl.BlockSpec((n, LANES), lambda i: (0, 0)),
                  pl.BlockSpec((n, LANES), lambda i: (0, 0))],
        out_specs=pl.BlockSpec((1, NSA_KV_HEADS, n, LANES), lambda i: (i, 0, 0, 0)),
        out_shape=jax.ShapeDtypeStruct((b, NSA_KV_HEADS, n, LANES), BF16),
        scratch_shapes=[pltpu.VMEM((n + 8, hid2), F32)],
        compiler_params=_cparams(("arbitrary",)),
        name="compress",
    )(rows4, pe_d, w1_bd, w2_bd, cos_t, sin_t)


def _nsa_kernel(q_ref, gate_ref, kc_ref, vc_ref, ovl_ref, ks_ref, vs_ref, kw_ref, vw_ref, o_ref, *, tq, tk, n_top):
    g = pl.program_id(1)
    qi = pl.program_id(2)
    t_len = ks_ref.shape[2]
    n_c = kc_ref.shape[2]
    n_slc = ovl_ref.shape[0]
    s0 = qi * tq
    zero = jnp.zeros((), BF16)
    lane = lax.broadcasted_iota(jnp.int32, (1, LANES), 1)
    head_lanes = [lane < HEAD_DIM, lane >= HEAD_DIM]
    q = q_ref[0]
    q_rows = jnp.concatenate(
        [jnp.where(head_lanes[h % 2], q[:, (h // 2) * LANES:(h // 2 + 1) * LANES], zero) for h in range(NSA_GROUP)],
        axis=0)
    q_pos = s0 + lax.broadcasted_iota(jnp.int32, (tq, 1), 0)
    q_pos4 = jnp.concatenate([q_pos] * NSA_GROUP, axis=0)

    def head_merge(o_rows):
        slabs = []
        for p in range(NSA_GROUP // 2):
            a = o_rows[(2 * p) * tq:(2 * p + 1) * tq]
            b2 = o_rows[(2 * p + 1) * tq:(2 * p + 2) * tq]
            slabs.append(jnp.where(head_lanes[0], a, b2))
        return jnp.concatenate(slabs, axis=1)

    kc = kc_ref[0, 0]
    vc = vc_ref[0, 0]
    s = _dot_nt(q_rows, kc)
    c_idx = lax.broadcasted_iota(jnp.int32, (1, n_c), 1)
    c_end = c_idx * CMP_STRIDE + (CMP_BLOCK - 1)
    c_mask = (c_end <= q_pos4) & (c_idx < n_c - 1)
    s = jnp.where(c_mask, s, NEG)
    m = jnp.max(s, axis=-1, keepdims=True)
    e = jnp.where(c_mask, jnp.exp(s - m), 0.0)
    p_cmp = e / jnp.maximum(jnp.sum(e, axis=-1, keepdims=True), 1.0)
    o_cmp = head_merge(_dot(p_cmp.astype(BF16), vc))

    p_sum = p_cmp[0:tq]
    for h in range(1, NSA_GROUP):
        p_sum = p_sum + p_cmp[h * tq:(h + 1) * tq]
    ovl = ovl_ref[...]
    p_hi, p_lo = _split2(p_sum)
    p_lo2 = (p_sum - p_hi.astype(F32) - p_lo.astype(F32)).astype(BF16)
    imp = _dot_nt(ovl, p_hi) + _dot_nt(ovl, p_lo) + _dot_nt(ovl, p_lo2)
    blk = lax.broadcasted_iota(jnp.int32, (n_slc, 1), 0)
    qp_l = s0 + lax.broadcasted_iota(jnp.int32, (1, tq), 1)
    cur = qp_l // SLC_BLOCK
    forced = (blk == 0) | (blk == cur) | (blk == cur - 1)
    imp = jnp.where(blk * SLC_BLOCK <= qp_l, imp + jnp.where(forced, FORCE_BONUS, 0.0), NEG)
    blk_full = lax.broadcasted_iota(jnp.int32, (n_slc, tq), 0)

    def pick(_, carry):
        imp_c, unsel = carry
        best = jnp.max(imp_c, axis=0, keepdims=True)
        first = jnp.min(jnp.where(imp_c == best, blk_full, n_slc), axis=0, keepdims=True)
        hit = blk_full == first
        return jnp.where(hit, -jnp.inf, imp_c), jnp.where(hit, 0.0, unsel)

    _, unsel_t = lax.fori_loop(0, n_top, pick, (imp, jnp.ones((n_slc, tq), F32)))
    unsel = unsel_t.T.astype(BF16)
    q_aug = jnp.concatenate([q_rows, jnp.concatenate([unsel] * NSA_GROUP, axis=0)], axis=1)

    blocks_per_tile = tk // SLC_BLOCK
    k_row_blk = lax.broadcasted_iota(jnp.int32, (tk, n_slc), 0) // SLC_BLOCK
    k_lane = lax.broadcasted_iota(jnp.int32, (tk, n_slc), 1)
    k_off = lax.broadcasted_iota(jnp.int32, (1, tk), 1)

    def slc_tile(j, carry, causal):
        m_run, l_run, acc = carry
        start = pl.multiple_of(j * tk, tk)
        k = ks_ref[0, 0, pl.ds(start, tk), :]
        v = vs_ref[0, 0, pl.ds(start, tk), :]
        onehot = jnp.where(k_lane == k_row_blk + j * blocks_per_tile, MASK_BIAS, 0.0).astype(BF16)
        k_aug = jnp.concatenate([k, onehot], axis=1)
        sc = _dot_nt(q_aug, k_aug)
        if causal:
            sc = jnp.where(start + k_off <= q_pos4, sc, MASK_BIAS)
        m_new = jnp.maximum(m_run, jnp.max(sc, axis=-1, keepdims=True))
        alpha = jnp.exp(m_run - m_new)
        pr = jnp.exp(sc - m_new)
        l_new = alpha * l_run + jnp.sum(pr, axis=-1, keepdims=True)
        acc = alpha * acc + _dot(pr.astype(BF16), v)
        return m_new, l_new, acc

    last = (s0 + tq - 1) // tk
    init = (jnp.full((NSA_GROUP * tq, 1), NEG, F32), jnp.zeros((NSA_GROUP * tq, 1), F32),
            jnp.zeros((NSA_GROUP * tq, LANES), F32))
    carry = lax.fori_loop(0, last, lambda j, c: slc_tile(j, c, False), init)
    _, l_fin, acc = slc_tile(last, carry, True)
    o_slc = head_merge(acc / jnp.maximum(l_fin, 1.0))

    band = min(WINDOW + tq, t_len)
    w_start = jnp.clip(s0 - WINDOW, 0, t_len - band)
    w_start = pl.multiple_of(w_start, tq)
    kw = kw_ref[0, 0, pl.ds(w_start, band), :]
    vw = vw_ref[0, 0, pl.ds(w_start, band), :]
    sw = _dot_nt(q_rows, kw)
    w_pos = w_start + lax.broadcasted_iota(jnp.int32, (1, band), 1)
    dist = q_pos4 - w_pos
    w_mask = (dist >= 0) & (dist < WINDOW)
    sw = jnp.where(w_mask, sw, NEG)
    mw = jnp.max(sw, axis=-1, keepdims=True)
    ew = jnp.where(w_mask, jnp.exp(sw - mw), 0.0)
    pw = ew / jnp.maximum(jnp.sum(ew, axis=-1, keepdims=True), 1.0)
    o_win = head_merge(_dot(pw.astype(BF16), vw))

    gates = gate_ref[0]
    g_hi, g_lo = _split2(gates)
    col = lax.broadcasted_iota(jnp.int32, (LANES, NSA_GROUP * HEAD_DIM), 0)
    head_of_lane = lax.broadcasted_iota(jnp.int32, (LANES, NSA_GROUP * HEAD_DIM), 1) // HEAD_DIM + g * NSA_GROUP
    out = jnp.zeros((tq, NSA_GROUP * HEAD_DIM), F32)
    for branch, o_b in enumerate((o_cmp, o_slc, o_win)):
        expand = (col == head_of_lane * 3 + branch).astype(BF16)
        out = out + (_dot(g_hi, expand) + _dot(g_lo, expand)) * o_b
    o_ref[0] = out


def _cmp_overlap_t(n_c, n_slc):
    cs = jnp.arange(n_c) * CMP_STRIDE
    ss = jnp.arange(n_slc) * SLC_BLOCK
    ok = (cs[None, :] < ss[:, None] + SLC_BLOCK) & (cs[None, :] + CMP_BLOCK > ss[:, None]) & (jnp.arange(n_c)[None, :] < n_c - 1)
    return ok.astype(BF16)


def _nsa_prompt(nq_b, gates, kc, vc, dup_b):
    b, t, _ = nq_b.shape
    n_c = kc.shape[2]
    n_slc = t // SLC_BLOCK
    tq = 128
    tk = min(1024, t)
    n_top = min(SLC_TOP_N, n_slc)
    gw = NSA_GROUP * HEAD_DIM
    kv_spec = lambda slot: pl.BlockSpec((1, 1, t, LANES), lambda i, g, j: (i, 0, 0, g * 4 + slot))
    dup4 = dup_b.reshape(b, 1, t, 8 * LANES)
    return pl.pallas_call(
        functools.partial(_nsa_kernel, tq=tq, tk=tk, n_top=n_top),
        grid=(b, NSA_KV_HEADS, t // tq),
        in_specs=[pl.BlockSpec((1, tq, gw), lambda i, g, j: (i, j, g)),
                  pl.BlockSpec((1, tq, LANES), lambda i, g, j: (i, j, 0)),
                  pl.BlockSpec((1, 1, n_c, LANES), lambda i, g, j: (i, g, 0, 0)),
                  pl.BlockSpec((1, 1, n_c, LANES), lambda i, g, j: (i, g, 0, 0)),
                  pl.BlockSpec((n_slc, n_c), lambda i, g, j: (0, 0)),
                  kv_spec(0), kv_spec(1), kv_spec(2), kv_spec(3)],
        out_specs=pl.BlockSpec((1, tq, gw), lambda i, g, j: (i, j, g)),
        out_shape=jax.ShapeDtypeStruct((b, t, NSA_WIDTH), F32),
        compiler_params=_cparams(("arbitrary", "arbitrary", "arbitrary")),
        name="nsa_prompt",
    )(nq_b, gates, kc, vc, _cmp_overlap_t(n_c, n_slc), dup4, dup4, dup4, dup4)


def _outproj_kernel(x_ref, osb_ref, onsa_ref, gate_ref, shift_ref, scale_ref, gsb_ref, gnsa_ref, gffn_ref,
                    wo_ref, wr_ref, br_ref, x1_ref, h_ref, logit_ref):
    o_sb = _rms(osb_ref[0], gsb_ref[...]).astype(BF16)
    o_nsa = _rms(onsa_ref[0], gnsa_ref[...]).astype(BF16)
    mixed = _dot(o_sb, wo_ref[0:SB_WIDTH, :]) + _dot(o_nsa, wo_ref[SB_WIDTH:, :])
    x1 = x_ref[0] + gate_ref[0] * mixed
    x1_ref[0] = x1
    h = _rms(x1, gffn_ref[...]) * (1.0 + scale_ref[0]) + shift_ref[0]
    h_ref[0] = h.astype(BF16)
    h_hi, h_lo = _split2(h)
    logit_ref[0] = _dot(h_hi, wr_ref[0]) + _dot(h_lo, wr_ref[0]) + _dot(h_hi, wr_ref[1]) + br_ref[...]


def _outproj(x, o_sb, o_nsa, gate, shift, scale, g_sb, g_nsa, g_ffn, w_out_b, w_router, b_router):
    b, t, d = x.shape
    tm = min(256, t)
    tmod = gate.shape[1]
    if tmod == 1:
        mod_spec = pl.BlockSpec((1, 1, d), lambda i, j: (i, 0, 0))
    else:
        mod_spec = pl.BlockSpec((1, tm, d), lambda i, j: (i, j, 0))
    wr = jnp.pad(w_router, ((0, 0), (0, LANES - N_EXPERTS)))
    wr_hi = wr.astype(BF16)
    wr_lo = (wr - wr_hi.astype(F32)).astype(BF16)
    wr2 = jnp.stack([wr_hi, wr_lo])
    br = jnp.pad(b_router, (0, LANES - N_EXPERTS)).reshape(1, LANES)
    row = lambda w: pl.BlockSpec((1, tm, w), lambda i, j: (i, j, 0))
    vec = lambda w: pl.BlockSpec((1, w), lambda i, j: (0, 0))
    return pl.pallas_call(
        _outproj_kernel,
        grid=(b, t // tm),
        in_specs=[row(d), row(SB_WIDTH), row(NSA_WIDTH), mod_spec, mod_spec, mod_spec,
                  vec(SB_WIDTH), vec(NSA_WIDTH), vec(d),
                  pl.BlockSpec((SB_WIDTH + NSA_WIDTH, d), lambda i, j: (0, 0)),
                  pl.BlockSpec((2, d, LANES), lambda i, j: (0, 0, 0)),
                  vec(LANES)],
        out_specs=[row(d), row(d), row(LANES)],
        out_shape=[jax.ShapeDtypeStruct((b, t, d), F32), jax.ShapeDtypeStruct((b, t, d), BF16),
                   jax.ShapeDtypeStruct((b, t, LANES), F32)],
        compiler_params=_cparams(("arbitrary", "arbitrary")),
        name="outproj",
    )(x, o_sb, o_nsa, gate, shift, scale, g_sb.reshape(1, -1), g_nsa.reshape(1, -1), g_ffn.reshape(1, -1),
      w_out_b, wr2, br)


def _moe_kernel(be_ref, xs_ref, gate_ref, w1_ref, b1_ref, w2_ref, b2_ref, o_ref):
    del be_ref
    d_ff = w2_ref.shape[1]
    gu = _dot(xs_ref[...], w1_ref[0]) + b1_ref[0]
    gl = jnp.minimum(gu[:, :d_ff], SWIGLU_LIMIT)
    u = jnp.clip(gu[:, d_ff:], -SWIGLU_LIMIT, SWIGLU_LIMIT)
    act = (u + 1.0) * gl * jax.nn.sigmoid(SWIGLU_ALPHA * gl)
    y = _dot(act.astype(BF16), w2_ref[0]) + b2_ref[0]
    o_ref[...] = y * gate_ref[...]


def _moe_ffn(xs, row_gate, block_expert, w1_b, b1, w2_b, b2, rb):
    rows, d = xs.shape
    n_blocks = rows // rb
    e, _, f2 = w1_b.shape
    grid_spec = pltpu.PrefetchScalarGridSpec(
        num_scalar_prefetch=1,
        grid=(n_blocks,),
        in_specs=[pl.BlockSpec((rb, d), lambda i, be: (i, 0)),
                  pl.BlockSpec((rb, 1), lambda i, be: (i, 0)),
                  pl.BlockSpec((1, d, f2), lambda i, be: (be[i], 0, 0)),
                  pl.BlockSpec((1, 1, f2), lambda i, be: (be[i], 0, 0)),
                  pl.BlockSpec((1, f2 // 2, d), lambda i, be: (be[i], 0, 0)),
                  pl.BlockSpec((1, 1, d), lambda i, be: (be[i], 0, 0))],
        out_specs=pl.BlockSpec((rb, d), lambda i, be: (i, 0)),
    )
    return pl.pallas_call(
        _moe_kernel,
        grid_spec=grid_spec,
        out_shape=jax.ShapeDtypeStruct((rows, d), F32),
        compiler_params=_cparams(("arbitrary",)),
        name="moe_ffn",
    )(block_expert, xs, row_gate.reshape(rows, 1), w1_b, b1.reshape(e, 1, f2), w2_b, b2.reshape(e, 1, d))


def _moe(h_b, logits, w1_b, b1, w2_b, b2):
    n, d = h_b.shape
    a = n * TOP_K
    top_val, top_idx = lax.top_k(logits, TOP_K)
    gate = jax.nn.softmax(top_val, axis=-1)
    rb = MOE_ROW_BLOCK if a >= MOE_ROW_BLOCK * N_EXPERTS else max(8, a // N_EXPERTS)
    n_blocks = -(-a // rb) + N_EXPERTS
    iota = jnp.arange(a, dtype=jnp.int32)
    e_sorted, order, g_sorted = lax.sort((top_idx.reshape(a).astype(jnp.int32), iota, gate.reshape(a)), num_keys=1)
    experts = jnp.arange(N_EXPERTS, dtype=jnp.int32)
    start = jnp.searchsorted(e_sorted, experts, side='left').astype(jnp.int32)
    counts = jnp.searchsorted(e_sorted, experts, side='right').astype(jnp.int32) - start
    padded = (counts + rb - 1) // rb * rb
    pad_end = jnp.cumsum(padded)
    pad_start = pad_end - padded
    block_expert = jnp.minimum(jnp.searchsorted(pad_end, jnp.arange(n_blocks) * rb, side='right'),
                               N_EXPERTS - 1).astype(jnp.int32)
    within = (jnp.arange(n_blocks, dtype=jnp.int32) * rb - pad_start[block_expert])[:, None] + jnp.arange(rb, dtype=jnp.int32)
    valid = within < counts[block_expert][:, None]
    src = jnp.where(valid, start[block_expert][:, None] + within, 0).reshape(-1)
    valid = valid.reshape(-1)
    row_tok = jnp.where(valid, order[src] // TOP_K, n)
    row_gate = jnp.where(valid, g_sorted[src], 0.0)
    xs = jnp.concatenate([h_b, jnp.zeros((1, d), h_b.dtype)])[row_tok]
    ys = _moe_ffn(xs, row_gate, block_expert, w1_b, b1, w2_b, b2, rb)
    shift = pad_start - start
    step = shift - jnp.concatenate([jnp.zeros((1,), shift.dtype), shift[:-1]])
    dest_sorted = iota + jnp.sum(jnp.where(iota[:, None] >= start[None, :], step[None, :], 0), axis=1).astype(jnp.int32)
    _, dest = lax.sort((order, dest_sorted), num_keys=1)
    return ys[dest].reshape(n, TOP_K * d)


def _final_kernel(x1_ref, y_ref, gate_ref, g_ref, o_ref):
    d = x1_ref.shape[2]
    y = y_ref[0, :, 0:d]
    for k in range(1, TOP_K):
        y = y + y_ref[0, :, k * d:(k + 1) * d]
    o_ref[0] = _rms(x1_ref[0] + gate_ref[0] * y, g_ref[...])


def _final(x1, y, gate, g_final):
    b, t, d = x1.shape
    tm = min(256, t)
    tmod = gate.shape[1]
    if tmod == 1:
        mod_spec = pl.BlockSpec((1, 1, d), lambda i, j: (i, 0, 0))
    else:
        mod_spec = pl.BlockSpec((1, tm, d), lambda i, j: (i, j, 0))
    row = pl.BlockSpec((1, tm, d), lambda i, j: (i, j, 0))
    return pl.pallas_call(
        _final_kernel,
        grid=(b, t // tm),
        in_specs=[row, pl.BlockSpec((1, tm, TOP_K * d), lambda i, j: (i, j, 0)), mod_spec,
                  pl.BlockSpec((1, d), lambda i, j: (0, 0))],
        out_specs=row,
        out_shape=jax.ShapeDtypeStruct((b, t, d), F32),
        compiler_params=_cparams(("arbitrary", "arbitrary")),
        name="final",
    )(x1, y, gate, g_final.reshape(1, d))


PAGES_PER_STEP = 8


def _pool_feature_major(cache_l):
    n_pool, page_rows = cache_l.shape[:2]
    return jnp.transpose(cache_l, (0, 2, 3, 4, 1)).reshape(n_pool, -1, page_rows)


def _page_specs(pg, n_steps, n_feat, page_rows, reverse):
    def spec(i):
        def index(b, s, pt):
            grp = (n_steps - 1 - s) if reverse else s
            return (pt[b, grp * pg + i], 0, 0)
        return pl.BlockSpec((1, n_feat, page_rows), index)
    return [spec(i) for i in range(pg)]


def _sb_sample_kernel(pt_ref, q_ref, kn_ref, vn_ref, *rest, pg, ts):
    del pt_ref
    pages = rest[:pg]
    o_ref, r_ref, acc_ref = rest[pg:]
    s = pl.program_id(1)
    n_rows = q_ref.shape[1]
    seg = 2 * LANES
    q = q_ref[0]
    tri = (lax.broadcasted_iota(jnp.int32, (seg, seg), 0) > lax.broadcasted_iota(jnp.int32, (seg, seg), 1)).astype(BF16)
    q_tok = lax.broadcasted_iota(jnp.int32, (n_rows, 1), 0) % ts

    def tile(k_t, v_t, r, acc, mask):
        nseg = k_t.shape[1] // seg
        z = _dot(q, k_t)
        neg_log_keep = _neg_log2_keep(z)
        if mask is not None:
            neg_log_keep = jnp.where(mask, neg_log_keep, 0.0)
        segs = [neg_log_keep[:, g * seg:(g + 1) * seg] for g in range(nseg)]
        stacked = segs[0] if nseg == 1 else jnp.concatenate(segs, axis=0)
        hi, lo = _split2(stacked)
        later = _dot(hi, tri) + _dot(lo, tri)
        offs = [None] * nseg
        off = r
        for g in reversed(range(nseg)):
            offs[g] = off
            off = off + jnp.sum(segs[g], axis=-1, keepdims=True)
        parts = [later[g * n_rows:(g + 1) * n_rows] + offs[g] for g in range(nseg)]
        cs = parts[0] if nseg == 1 else jnp.concatenate(parts, axis=1)
        a = jnp.exp2(z - neg_log_keep - cs)
        if mask is not None:
            a = jnp.where(mask, a, 0.0)
        return off, acc + _dot_nt(a.astype(BF16), v_t)

    @pl.when(s == 0)
    def _():
        key = lax.broadcasted_iota(jnp.int32, (1, seg), 1)
        r, acc = tile(kn_ref[0], vn_ref[0], jnp.zeros((n_rows, 1), F32), jnp.zeros((n_rows, SB_WIDTH), F32),
                      key < q_tok)
        r_ref[...] = jnp.broadcast_to(r, r_ref.shape)
        acc_ref[...] = acc

    k_t = jnp.concatenate([p[0, 0:SB_WIDTH, :] for p in pages], axis=1).astype(BF16)
    v_t = jnp.concatenate([p[0, SB_WIDTH:2 * SB_WIDTH, :] for p in pages], axis=1).astype(BF16)
    r, acc = tile(k_t, v_t, r_ref[:, 0:1], acc_ref[...], None)
    r_ref[...] = jnp.broadcast_to(r, r_ref.shape)
    acc_ref[...] = acc

    @pl.when(s == pl.num_programs(1) - 1)
    def _():
        lane_head = lax.broadcasted_iota(jnp.int32, (n_rows, SB_WIDTH), 1) // HEAD_DIM
        row_head = lax.broadcasted_iota(jnp.int32, (n_rows, SB_WIDTH), 0) // ts
        own = jnp.where(lane_head == row_head, acc, 0.0)
        out = own[0:ts]
        for h in range(1, N_HEADS_SB):
            out = out + own[h * ts:(h + 1) * ts]
        o_ref[0] = out


def _sb_sample(sq_b, sbkv_new_b, pool, page_table):
    bs, ts, _ = sq_b.shape
    n_pages = page_table.shape[1]
    page_rows = pool.shape[2]
    pg = min(PAGES_PER_STEP, n_pages)
    n_steps = n_pages // pg
    n_rows = N_HEADS_SB * ts
    seg = 2 * LANES
    eye = jnp.eye(N_HEADS_SB, dtype=sq_b.dtype)
    q_rows = jnp.einsum('bqhd,hg->bhqgd', sq_b.reshape(bs, ts, N_HEADS_SB, HEAD_DIM), eye).reshape(bs, n_rows, SB_WIDTH)
    new = jnp.pad(sbkv_new_b, ((0, 0), (0, seg - ts), (0, 0))).transpose(0, 2, 1)
    k_new, v_new = new[:, :SB_WIDTH], new[:, SB_WIDTH:]
    per_b = lambda rows, w: pl.BlockSpec((1, rows, w), lambda b, s, pt: (b, 0, 0))
    grid_spec = pltpu.PrefetchScalarGridSpec(
        num_scalar_prefetch=1,
        grid=(bs, n_steps),
        in_specs=[per_b(n_rows, SB_WIDTH), per_b(SB_WIDTH, seg), per_b(SB_WIDTH, seg)]
        + _page_specs(pg, n_steps, 2 * SB_WIDTH, page_rows, True),
        out_specs=per_b(ts, SB_WIDTH),
        scratch_shapes=[pltpu.VMEM((n_rows, LANES), F32), pltpu.VMEM((n_rows, SB_WIDTH), F32)],
    )
    return pl.pallas_call(
        functools.partial(_sb_sample_kernel, pg=pg, ts=ts),
        grid_spec=grid_spec,
        out_shape=jax.ShapeDtypeStruct((bs, ts, SB_WIDTH), F32),
        compiler_params=_cparams(("arbitrary", "arbitrary")),
        name="sb_sample",
    )(page_table, q_rows, k_new, v_new, *([pool] * pg))


def _nsa_stage_kernel(pt_ref, new_ref, pek_ref, w1k_ref, w2k_ref, pev_ref, w1v_ref, w2v_ref, cos_ref, sin_ref, *rest,
                      pg, n_steps):
    del pt_ref
    pages = rest[:pg]
    kc_ref, vc_ref, lkv_ref, krows_ref, vrows_ref, bot_ref = rest[pg:]
    s = pl.program_id(1)
    n_cp = krows_ref.shape[0] // CMP_STRIDE
    page_rows = pages[0].shape[2]
    n_past_rows = n_steps * pg * page_rows
    cw = 2 * KV_WIDTH

    @pl.when(s == 0)
    def _():
        tail = n_cp * CMP_STRIDE - n_past_rows - CMP_STRIDE
        for ref, lo in ((krows_ref, 0), (vrows_ref, KV_WIDTH)):
            ref[pl.ds(n_past_rows, CMP_STRIDE), :] = new_ref[0, :, lo:lo + KV_WIDTH]
            ref[pl.ds(n_past_rows + CMP_STRIDE, tail), :] = jnp.zeros((tail, KV_WIDTH), F32)

    for i, p in enumerate(pages):
        cmp_rows = p[0, 0:cw, :].T
        base = pl.multiple_of((s * pg + i) * page_rows, page_rows)
        krows_ref[pl.ds(base, page_rows), :] = cmp_rows[:, 0:KV_WIDTH]
        vrows_ref[pl.ds(base, page_rows), :] = cmp_rows[:, KV_WIDTH:cw]
        lkv_ref[0, 0, :, i * page_rows:(i + 1) * page_rows] = p[0, cw:2 * cw, :].astype(BF16)

    @pl.when(s == n_steps - 1)
    def _():
        def store_k(g, y):
            kc_ref[0, g] = y

        def store_v(g, y):
            vc_ref[0, g] = y

        _compress_mlp(lambda r: krows_ref[pl.ds(r, n_cp, stride=CMP_STRIDE), :], n_cp, pek_ref, w1k_ref, w2k_ref,
                      bot_ref, (cos_ref, sin_ref), store_k)
        _compress_mlp(lambda r: vrows_ref[pl.ds(r, n_cp, stride=CMP_STRIDE), :], n_cp, pev_ref, w1v_ref, w2v_ref,
                      bot_ref, None, store_v)


def _nsa_stage(pool, page_table, new_cmp, cmp_k, cmp_v):
    bs, n_pages = page_table.shape
    page_rows = pool.shape[2]
    pg = min(PAGES_PER_STEP, n_pages)
    n_steps = n_pages // pg
    cpp = page_rows // CMP_STRIDE
    n_past = n_pages * cpp
    n_cp = n_past + 8
    pe_k, w1k, w2k = _prep_cmp_weights(*cmp_k)
    pe_v, w1v, w2v = _prep_cmp_weights(*cmp_v)
    hid2 = w1k.shape[2]
    cos_t, sin_t = _rope_tables(jnp.arange(n_cp) * CMP_STRIDE + CMP_BLOCK - 1)
    cw = 2 * KV_WIDTH
    const = lambda shape: pl.BlockSpec(shape, lambda b, s, pt: (0,) * len(shape))
    w_specs = [const((CMP_BLOCK, LANES)), const((CMP_BLOCK // 2, 2 * LANES, hid2)), const((hid2, 2 * LANES))]
    out_c = pl.BlockSpec((1, NSA_KV_HEADS, n_cp, LANES), lambda b, s, pt: (b, 0, 0, 0))
    grid_spec = pltpu.PrefetchScalarGridSpec(
        num_scalar_prefetch=1,
        grid=(bs, n_steps),
        in_specs=[pl.BlockSpec((1, CMP_STRIDE, cw), lambda b, s, pt: (b, 0, 0))] + w_specs + w_specs
        + [const((n_cp, LANES)), const((n_cp, LANES))] + _page_specs(pg, n_steps, 2 * cw, page_rows, False),
        out_specs=[out_c, out_c, pl.BlockSpec((1, 1, cw, pg * page_rows), lambda b, s, pt: (b, s, 0, 0))],
        scratch_shapes=[pltpu.VMEM((n_cp * CMP_STRIDE, KV_WIDTH), F32), pltpu.VMEM((n_cp * CMP_STRIDE, KV_WIDTH), F32),
                        pltpu.VMEM((n_cp + 8, hid2), F32)],
    )
    c_shape = jax.ShapeDtypeStruct((bs, NSA_KV_HEADS, n_cp, LANES), BF16)
    return pl.pallas_call(
        functools.partial(_nsa_stage_kernel, pg=pg, n_steps=n_steps),
        grid_spec=grid_spec,
        out_shape=[c_shape, c_shape, jax.ShapeDtypeStruct((bs, n_steps, cw, pg * page_rows), BF16)],
        compiler_params=_cparams(("arbitrary", "arbitrary")),
        name="nsa_stage",
    )(page_table, new_cmp, pe_k, w1k, w2k, pe_v, w1v, w2v, cos_t, sin_t, *([pool] * pg))


def _nsa_sample_kernel(q_ref, gate_ref, kc_ref, vc_ref, ovl_ref, lkv_ref, lkvn_ref, win_ref, winn_ref, o_ref, *,
                       ts, past_len, n_cmp, n_slc, n_top, tk):
    q = q_ref[0]
    n_rows = q.shape[0]
    gr = n_rows // NSA_KV_HEADS
    n_cp = kc_ref.shape[2]
    n_sp = ovl_ref.shape[1]
    q_tok = lax.broadcasted_iota(jnp.int32, (n_rows, 1), 0) % ts
    q_pos = past_len + q_tok

    c_idx = lax.broadcasted_iota(jnp.int32, (1, n_cp), 1)
    c_mask = (c_idx * CMP_STRIDE + (CMP_BLOCK - 1) <= q_pos) & (c_idx < n_cmp)
    s = jnp.concatenate([_dot_nt(q[g * gr:(g + 1) * gr], kc_ref[0, g]) for g in range(NSA_KV_HEADS)], axis=0)
    s = jnp.where(c_mask, s, NEG)
    m = jnp.max(s, axis=-1, keepdims=True)
    e = jnp.where(c_mask, jnp.exp(s - m), 0.0)
    p_cmp = e / jnp.maximum(jnp.sum(e, axis=-1, keepdims=True), 1.0)
    p_b = p_cmp.astype(BF16)
    o_cmp = jnp.concatenate([_dot(p_b[g * gr:(g + 1) * gr], vc_ref[0, g]) for g in range(NSA_KV_HEADS)], axis=0)

    sums = []
    for g in range(NSA_KV_HEADS):
        acc = p_cmp[g * gr:g * gr + ts]
        for h in range(1, NSA_GROUP):
            acc = acc + p_cmp[g * gr + h * ts:g * gr + (h + 1) * ts]
        sums.append(acc)
    p_sum = jnp.concatenate(sums, axis=0)
    n_imp = p_sum.shape[0]
    p_hi, p_lo = _split2(p_sum)
    p_lo2 = (p_sum - p_hi.astype(F32) - p_lo.astype(F32)).astype(BF16)
    ovl = ovl_ref[...]
    imp = _dot(p_hi, ovl) + _dot(p_lo, ovl) + _dot(p_lo2, ovl)
    blk = lax.broadcasted_iota(jnp.int32, (n_imp, n_sp), 1)
    qp = past_len + lax.broadcasted_iota(jnp.int32, (n_imp, 1), 0) % ts
    cur = qp // SLC_BLOCK
    forced = (blk == 0) | (blk == cur) | (blk == cur - 1)
    imp = jnp.where(blk * SLC_BLOCK <= qp, imp + jnp.where(forced, FORCE_BONUS, 0.0), NEG)
    imp = jnp.where(blk < n_slc, imp, -jnp.inf)

    def pick(_, carry):
        imp_c, unsel = carry
        best = jnp.max(imp_c, axis=-1, keepdims=True)
        first = jnp.min(jnp.where(imp_c == best, blk, n_sp), axis=-1, keepdims=True)
        hit = blk == first
        return jnp.where(hit, -jnp.inf, imp_c), jnp.where(hit, 0.0, unsel)

    _, unsel = lax.fori_loop(0, n_top, pick, (imp, jnp.ones((n_imp, n_sp), F32)))
    unsel = unsel.astype(BF16)
    unsel_rows = jnp.concatenate([unsel[g * ts:(g + 1) * ts] for g in range(NSA_KV_HEADS) for _ in range(NSA_GROUP)],
                                 axis=0)
    q_aug = jnp.concatenate([q, unsel_rows], axis=1)

    def update(carry, sc, v_t):
        m_run, l_run, acc = carry
        m_new = jnp.maximum(m_run, jnp.max(sc, axis=-1, keepdims=True))
        alpha = jnp.exp(m_run - m_new)
        pr = jnp.exp(sc - m_new)
        return (m_new, alpha * l_run + jnp.sum(pr, axis=-1, keepdims=True),
                alpha * acc + _dot_nt(pr.astype(BF16), v_t))

    def bias_rows(first_pos, n_keys):
        key_blk = (first_pos + lax.broadcasted_iota(jnp.int32, (n_sp, n_keys), 1)) // SLC_BLOCK
        return jnp.where(lax.broadcasted_iota(jnp.int32, (n_sp, n_keys), 0) == key_blk, MASK_BIAS, 0.0).astype(BF16)

    def slc_tile(j, carry):
        k_t = lkv_ref[0, j, 0:KV_WIDTH, :]
        v_t = lkv_ref[0, j, KV_WIDTH:2 * KV_WIDTH, :]
        sc = _dot(q_aug, jnp.concatenate([k_t, bias_rows(j * tk, tk)], axis=0))
        return update(carry, sc, v_t)

    init = (jnp.full((n_rows, 1), NEG, F32), jnp.zeros((n_rows, 1), F32), jnp.zeros((n_rows, KV_WIDTH), F32))
    carry = lax.fori_loop(0, past_len // tk, slc_tile, init)
    n_new = lkvn_ref.shape[2]
    new_idx = lax.broadcasted_iota(jnp.int32, (1, n_new), 1)
    kn_t = lkvn_ref[0, 0:KV_WIDTH, :]
    vn_t = lkvn_ref[0, KV_WIDTH:2 * KV_WIDTH, :]
    sc = _dot(q_aug, jnp.concatenate([kn_t, bias_rows(past_len, n_new)], axis=0))
    sc = jnp.where(new_idx <= q_tok, sc, MASK_BIAS)
    _, l_fin, acc = update(carry, sc, vn_t)
    o_slc = acc / jnp.maximum(l_fin, 1.0)

    wl = win_ref.shape[1]
    kw = win_ref[0, :, 0:KV_WIDTH].astype(BF16)
    vw = win_ref[0, :, KV_WIDTH:2 * KV_WIDTH].astype(BF16)
    kwn = winn_ref[0, :, 0:KV_WIDTH]
    vwn = winn_ref[0, :, KV_WIDTH:2 * KV_WIDTH]
    w_pos = past_len - wl + lax.broadcasted_iota(jnp.int32, (1, wl), 1)
    dist = q_pos - w_pos
    mask_w = (w_pos >= 0) & (dist >= 0) & (dist < WINDOW)
    dist_n = q_tok - new_idx
    mask_n = (dist_n >= 0) & (dist_n < WINDOW)
    mask = jnp.concatenate([mask_w, mask_n], axis=1)
    sw = jnp.where(mask, jnp.concatenate([_dot_nt(q, kw), _dot_nt(q, kwn)], axis=1), NEG)
    mw = jnp.max(sw, axis=-1, keepdims=True)
    ew = jnp.where(mask, jnp.exp(sw - mw), 0.0)
    pw = (ew / jnp.maximum(jnp.sum(ew, axis=-1, keepdims=True), 1.0)).astype(BF16)
    o_win = _dot(pw[:, :wl], vw) + _dot(pw[:, wl:], vwn)

    gates = gate_ref[0]
    o_ref[0] = gates[:, 0:1] * o_cmp + gates[:, 1:2] * o_slc + gates[:, 2:3] * o_win


def _nsa_sample(nq_b, gates, kc, vc, lkv, nsakv_new, win_state, winkv_new, past_len):
    bs, ts, _ = nq_b.shape
    n_cp = kc.shape[2]
    n_cmp = (past_len + ts + CMP_STRIDE - 1) // CMP_STRIDE - 1
    n_slc = -(-(past_len + ts) // SLC_BLOCK)
    n_sp = -(-n_slc // LANES) * LANES
    n_top = min(SLC_TOP_N, n_slc)
    tk = lkv.shape[3]
    n_rows = N_HEADS_NSA * ts
    q5 = nq_b.reshape(bs, ts, NSA_KV_HEADS, NSA_GROUP, HEAD_DIM).transpose(0, 2, 3, 1, 4)
    zq = jnp.zeros_like(q5[:, 0])
    q_rows = jnp.stack([jnp.concatenate([q5[:, 0], zq], axis=-1), jnp.concatenate([zq, q5[:, 1]], axis=-1)], axis=1)
    q_rows = q_rows.reshape(bs, n_rows, KV_WIDTH)
    g_rows = gates.reshape(bs, ts, NSA_KV_HEADS, NSA_GROUP, 3).transpose(0, 2, 3, 1, 4).reshape(bs, n_rows, 3)
    g_rows = jnp.pad(g_rows, ((0, 0), (0, 0), (0, LANES - 3)))
    pad_new = lambda x: jnp.pad(x, ((0, 0), (0, LANES - ts), (0, 0))).astype(BF16)
    lkv_new = pad_new(nsakv_new[:, :, 2 * KV_WIDTH:]).transpose(0, 2, 1)
    win_new = pad_new(winkv_new)
    cs = jnp.arange(n_cp) * CMP_STRIDE
    ss = jnp.arange(n_sp) * SLC_BLOCK
    ovl = ((cs[:, None] < ss[None, :] + SLC_BLOCK) & (cs[:, None] + CMP_BLOCK > ss[None, :])
           & (jnp.arange(n_cp)[:, None] < n_cmp) & (jnp.arange(n_sp)[None, :] < n_slc)).astype(BF16)
    wl = win_state.shape[1]
    per_b = lambda shape: pl.BlockSpec((1,) + shape, lambda b: (b,) + (0,) * len(shape))
    out = pl.pallas_call(
        functools.partial(_nsa_sample_kernel, ts=ts, past_len=past_len, n_cmp=n_cmp, n_slc=n_slc, n_top=n_top, tk=tk),
        grid=(bs,),
        in_specs=[per_b((n_rows, KV_WIDTH)), per_b((n_rows, LANES)),
                  per_b((NSA_KV_HEADS, n_cp, LANES)), per_b((NSA_KV_HEADS, n_cp, LANES)),
                  pl.BlockSpec((n_cp, n_sp), lambda b: (0, 0)),
                  per_b(lkv.shape[1:]), per_b((2 * KV_WIDTH, LANES)),
                  per_b((wl, 2 * KV_WIDTH)), per_b((LANES, 2 * KV_WIDTH))],
        out_specs=per_b((n_rows, KV_WIDTH)),
        out_shape=jax.ShapeDtypeStruct((bs, n_rows, KV_WIDTH), F32),
        compiler_params=_cparams(("arbitrary",)),
        name="nsa_sample",
    )(q_rows, g_rows, kc, vc, ovl, lkv, lkv_new, win_state, win_new)
    o5 = out.reshape(bs, NSA_KV_HEADS, NSA_GROUP, ts, NSA_KV_HEADS, HEAD_DIM)
    o = jnp.stack([o5[:, g, :, :, g] for g in range(NSA_KV_HEADS)], axis=1)
    return o.transpose(0, 3, 1, 2, 4).reshape(bs, ts, NSA_WIDTH)


def _mix_out(x, o_sb, o_nsa, mods, lw, g_final):
    (gate_attn, shift_ffn, scale_ffn, gate_ffn) = mods
    (g_sb_out, g_nsa_out, w_out_b, g_ffn, w_router, b_router, w1_b, b_e1, w2_b, b_e2) = lw
    b, t, d = x.shape
    x1, h_b, logits = _outproj(x, o_sb, o_nsa, gate_attn, shift_ffn, scale_ffn, g_sb_out, g_nsa_out, g_ffn,
                               w_out_b, w_router, b_router)
    y = _moe(h_b.reshape(b * t, d), logits.reshape(b * t, LANES)[:, :N_EXPERTS], w1_b, b_e1, w2_b, b_e2)
    return _final(x1, y.reshape(b, t, TOP_K * d), gate_ffn, g_final)


def kernel(x_prompt, x_sample, c_prompt, c_sample, cache_sb_kv, cache_nsa_kv, state_win_kv, page_table, w_ada, b_ada, g_attn, w_in, g_sb_out, g_nsa_out, w_out, pe_k, w_ck1, w_ck2, pe_v, w_cv1, w_cv2, g_ffn, w_router, b_router, w_e1, b_e1, w_e2, b_e2, g_final):
    depth = w_ada.shape[0]
    assert depth == 1
    l = 0
    bp, t, d = x_prompt.shape
    bs, ts, _ = x_sample.shape

    w_all = _prep_w_in(w_in[l])
    w_out_b = w_out[l].astype(BF16)
    w1_b = w_e1[l].astype(BF16)
    w2_b = w_e2[l].astype(BF16)
    out_lw = (g_sb_out[l], g_nsa_out[l], w_out_b, g_ffn[l], w_router[l], b_router[l], w1_b, b_e1[l], w2_b, b_e2[l])

    n_c = bp + bs
    n_c_pad = -(-n_c // 8) * 8
    c_all = jnp.pad(jnp.concatenate([c_prompt, c_sample]), ((0, n_c_pad - n_c), (0, 0)))
    mod_all = _adaln(c_all, w_ada[l].astype(BF16), b_ada[l]).reshape(n_c_pad, 6, d)
    mp = [mod_all[:bp, i][:, None, :] for i in range(6)]
    ms = [jnp.repeat(mod_all[bp:n_c, i], ts, axis=0)[None] for i in range(6)]

    pos = jnp.arange(t)
    sq_b, sbkv, sbkv_b, nq_b, nsakv, winkv, dup_b, gates = _inproj(x_prompt, mp[0], mp[1], g_attn[l], w_all, pos)
    o_sb = _sb_prompt(sq_b, sbkv_b)
    kc = _compress(nsakv, 0, pe_k[l], w_ck1[l], w_ck2[l], True)
    vc = _compress(nsakv, 1, pe_v[l], w_cv1[l], w_cv2[l], False)
    o_nsa = _nsa_prompt(nq_b, gates, kc, vc, dup_b)
    y_prompt = _mix_out(x_prompt, o_sb, o_nsa, (mp[2], mp[3], mp[4], mp[5]), out_lw, g_final)
    keep = min(WINDOW, t)
    sb_kv_prompt = sbkv.reshape(1, bp, t, 2, N_HEADS_SB, HEAD_DIM)
    nsa_kv_prompt = nsakv.reshape(1, bp, t, 4, NSA_KV_HEADS, HEAD_DIM)
    win_kv_prompt = winkv[:, t - keep:].reshape(1, bp, keep, 2, NSA_KV_HEADS, HEAD_DIM)

    past_len = page_table.shape[1] * cache_sb_kv.shape[2]
    q_pos = past_len + jnp.arange(ts)
    pos_rows = jnp.tile(q_pos, bs)
    xs_flat = x_sample.reshape(1, bs * ts, d)
    s_sq_b, s_sbkv, s_sbkv_b, s_nq_b, s_nsakv, s_winkv, _, s_gates = _inproj(xs_flat, ms[0], ms[1], g_attn[l], w_all,
                                                                              pos_rows)
    per_seq = lambda a: a.reshape(bs, ts, a.shape[-1])
    sb_pool = _pool_feature_major(cache_sb_kv[l])
    nsa_pool = _pool_feature_major(cache_nsa_kv[l])
    o_sb_s = _sb_sample(per_seq(s_sq_b), per_seq(s_sbkv_b), sb_pool, page_table)
    nsakv_s = per_seq(s_nsakv)
    winkv_s = per_seq(s_winkv)
    new_cmp = jnp.pad(nsakv_s[:, :, :2 * KV_WIDTH], ((0, 0), (0, CMP_STRIDE - ts), (0, 0)))
    kc_s, vc_s, lkv = _nsa_stage(nsa_pool, page_table, new_cmp, (pe_k[l], w_ck1[l], w_ck2[l]),
                                 (pe_v[l], w_cv1[l], w_cv2[l]))
    win_state = state_win_kv[l].reshape(bs, -1, 2 * KV_WIDTH)
    o_nsa_s = _nsa_sample(per_seq(s_nq_b), per_seq(s_gates)[:, :, :N_GATES], kc_s, vc_s, lkv, nsakv_s, win_state,
                          winkv_s, past_len)
    y_sample = _mix_out(xs_flat, o_sb_s.reshape(1, bs * ts, SB_WIDTH), o_nsa_s.reshape(1, bs * ts, NSA_WIDTH),
                        (ms[2], ms[3], ms[4], ms[5]), out_lw, g_final).reshape(bs, ts, d)
    sb_kv_sample = per_seq(s_sbkv).reshape(1, bs, ts, 2, N_HEADS_SB, HEAD_DIM)
    nsa_kv_sample = nsakv_s.reshape(1, bs, ts, 4, NSA_KV_HEADS, HEAD_DIM)
    win_kv_sample = jnp.concatenate([win_state, winkv_s], axis=1)[:, ts:].reshape(1, bs, -1, 2, NSA_KV_HEADS, HEAD_DIM)
    return (y_prompt, y_sample, sb_kv_prompt, nsa_kv_prompt, win_kv_prompt, sb_kv_sample, nsa_kv_sample, win_kv_sample)
```
